```python
import jax
import jax.numpy as jnp
from jax import lax
import numpy as np

D_MODEL = 1024
BATCH = 4
SEQ = 8192
DEPTH = 1

GMLP_WIDTH = D_MODEL // 2
GMLP_GROUPS = 4
GMLP_CH = GMLP_WIDTH // GMLP_GROUPS
GMLP_CHUNK = 128
ATT_HEADS = 8
HEAD_DIM = 64
ATT_WIDTH = ATT_HEADS * HEAD_DIM
ROPE_DIM = HEAD_DIM // 4
ROPE_THETA = 500000.0
MOBA_BLOCK = 256
MOBA_TOPK = 3
Q_BLOCK = 128
N_GROUPS = 4
EXPERTS_PER_GROUP = 8
N_EXPERTS = N_GROUPS * EXPERTS_PER_GROUP
TOP_K_EXPERT = 2
D_EXPERT = D_MODEL // 2
DISPATCH_BLOCK = 256
N_MOD = 6
PROJ_COLS = 2 * GMLP_WIDTH + 3 * ATT_WIDTH + 2 * D_MODEL
EPS = 1e-6

kernel_name = 'hybrid_gmlp_moba_hmoe_block'


def rmsnorm(x, g):
    xf = x.astype(jnp.float32)
    y = xf * lax.rsqrt(jnp.mean(xf * xf, axis=-1, keepdims=True) + EPS)
    return (y * g.astype(jnp.float32)).astype(x.dtype)


def layernorm(x, g, b):
    xf = x.astype(jnp.float32)
    mu = jnp.mean(xf, axis=-1, keepdims=True)
    var = jnp.mean(jnp.square(xf - mu), axis=-1, keepdims=True)
    y = (xf - mu) * lax.rsqrt(var + EPS) * g.astype(jnp.float32) + b.astype(jnp.float32)
    return y.astype(x.dtype)


def partial_rotary(t, pos):
    half = ROPE_DIM // 2
    inv_freq = jnp.power(ROPE_THETA, -jnp.arange(half, dtype=jnp.float32) * 2.0 / ROPE_DIM)
    ang = pos.astype(jnp.float32)[:, None] * inv_freq[None, :]
    cos = jnp.cos(ang)[None, :, None, :]
    sin = jnp.sin(ang)[None, :, None, :]
    t1 = t[..., :half].astype(jnp.float32)
    t2 = t[..., half:ROPE_DIM].astype(jnp.float32)
    rot = jnp.concatenate([t1 * cos - t2 * sin, t2 * cos + t1 * sin], axis=-1).astype(t.dtype)
    return jnp.concatenate([rot, t[..., ROPE_DIM:]], axis=-1)


def spatial_gating_mixer(p, ln_g, ln_b, w_spatial, b_spatial):
    B, S, _ = p.shape
    z = jax.nn.gelu(p)
    u, v = z[..., :GMLP_WIDTH], z[..., GMLP_WIDTH:]
    v = layernorm(v, ln_g, ln_b)
    v = v.reshape(B, S // GMLP_CHUNK, GMLP_CHUNK, GMLP_GROUPS, GMLP_CH)
    causal = jnp.tril(jnp.ones((GMLP_CHUNK, GMLP_CHUNK), dtype=bool))
    w = jnp.where(causal[None], w_spatial, 0)
    sv = jnp.einsum('gts,bnsgc->bntgc', w, v) + b_spatial.T[None, None, :, :, None]
    return u * sv.reshape(B, S, GMLP_WIDTH)


def moba_attention(q, k, v):
    B, S, H, Dh = q.shape
    S_pad = -(-S // MOBA_BLOCK) * MOBA_BLOCK
    if S_pad != S:
        padw = ((0, 0), (0, S_pad - S), (0, 0), (0, 0))
        q, k, v = jnp.pad(q, padw), jnp.pad(k, padw), jnp.pad(v, padw)
    nb = S_pad // MOBA_BLOCK
    nqb = S_pad // Q_BLOCK
    qpb = MOBA_BLOCK // Q_BLOCK
    ksel = min(MOBA_TOPK, nb)
    q = q.transpose(0, 2, 1, 3) * (Dh ** -0.5)
    k = k.transpose(0, 2, 1, 3)
    v = v.transpose(0, 2, 1, 3)
    k_blocks = k.reshape(B, H, nb, MOBA_BLOCK, Dh)
    v_blocks = v.reshape(B, H, nb, MOBA_BLOCK, Dh)
    k_mean = jnp.mean(k_blocks.astype(jnp.float32), axis=3)
    gate = jnp.einsum('bhtd,bhnd->bhtn', q.astype(jnp.float32), k_mean)
    own_blk = jnp.arange(S_pad) // MOBA_BLOCK
    past = jnp.arange(nb)[None, :] < own_blk[:, None]
    gate = jnp.where(past, gate, -jnp.inf)
    _, sel = lax.top_k(gate, ksel)

    q_steps = q.reshape(B, H, nqb, Q_BLOCK, Dh).transpose(0, 2, 1, 3, 4).reshape(B * nqb, H, Q_BLOCK, Dh)
    sel_steps = sel.reshape(B, H, nqb, Q_BLOCK, ksel).transpose(0, 2, 1, 3, 4).reshape(B * nqb, H, Q_BLOCK, ksel)
    b_ids = jnp.repeat(jnp.arange(B), nqb)
    qb_ids = jnp.tile(jnp.arange(nqb), B)

    def step(args):
        b, qi, q_blk, sel_blk = args
        kb = k_blocks[b]
        vb = v_blocks[b]
        k_sel = jax.vmap(lambda kh, ih: kh[ih])(kb, sel_blk)
        v_sel = jax.vmap(lambda vh, ih: vh[ih])(vb, sel_blk)
        j = qi // qpb
        k_own = lax.dynamic_index_in_dim(kb, j, axis=1, keepdims=False)
        v_own = lax.dynamic_index_in_dim(vb, j, axis=1, keepdims=False)
        q_pos = qi * Q_BLOCK + jnp.arange(Q_BLOCK)
        k_pos = j * MOBA_BLOCK + jnp.arange(MOBA_BLOCK)
        valid = jnp.arange(ksel) < j
        s_sel = jnp.einsum('hqd,hqnld->hqnl', q_blk, k_sel).astype(jnp.float32)
        s_sel = jnp.where(valid[None, None, :, None], s_sel, -jnp.inf).reshape(H, Q_BLOCK, ksel * MOBA_BLOCK)
        s_own = jnp.einsum('hqd,hld->hql', q_blk, k_own).astype(jnp.float32)
        s_own = jnp.where(k_pos[None, None, :] <= q_pos[None, :, None], s_own, -jnp.inf)
        prob = jax.nn.softmax(jnp.concatenate([s_sel, s_own], axis=-1), axis=-1).astype(v.dtype)
        p_sel = prob[..., :ksel * MOBA_BLOCK].reshape(H, Q_BLOCK, ksel, MOBA_BLOCK)
        p_own = prob[..., ksel * MOBA_BLOCK:]
        return jnp.einsum('hqnl,hqnld->hqd', p_sel, v_sel) + jnp.einsum('hql,hld->hqd', p_own, v_own)

    out = lax.map(step, (b_ids, qb_ids, q_steps, sel_steps))
    out = out.reshape(B, nqb, H, Q_BLOCK, Dh).transpose(0, 1, 3, 2, 4).reshape(B, S_pad, H * Dh)
    return out[:, :S]


def hierarchical_moe(h, w_rg, b_rg, w_re, b_re, w_gate, w_up, w_down):
    B, S, D = h.shape
    T = B * S
    xt = h.reshape(T, D)
    g_prob = jax.nn.softmax((xt @ w_rg + b_rg).astype(jnp.float32), axis=-1)
    g_p, g_idx = lax.top_k(g_prob, 1)
    e_logits = (jnp.einsum('td,dge->tge', xt, w_re) + b_re).astype(jnp.float32)
    e_logits = jnp.take_along_axis(e_logits, g_idx[:, :, None], axis=1)[:, 0]
    e_prob = jax.nn.softmax(e_logits, axis=-1)
    e_p, e_idx = lax.top_k(e_prob, TOP_K_EXPERT)
    weights = g_p * e_p / jnp.sum(e_p, axis=-1, keepdims=True)
    expert = g_idx * EXPERTS_PER_GROUP + e_idx

    A = T * TOP_K_EXPERT
    flat_e = expert.reshape(A)
    flat_tok = jnp.repeat(jnp.arange(T), TOP_K_EXPERT)
    flat_w = weights.reshape(A)
    order = jnp.argsort(flat_e)
    e_sorted = flat_e[order]
    tok_sorted = flat_tok[order]
    w_sorted = flat_w[order]
    counts = jnp.bincount(flat_e, length=N_EXPERTS)
    padded = (counts + DISPATCH_BLOCK - 1) // DISPATCH_BLOCK * DISPATCH_BLOCK
    start = jnp.cumsum(counts) - counts
    pend = jnp.cumsum(padded)
    pstart = pend - padded
    dest = pstart[e_sorted] + jnp.arange(A) - start[e_sorted]
    n_blocks = -(-A // DISPATCH_BLOCK) + N_EXPERTS
    P = n_blocks * DISPATCH_BLOCK
    x_buf = jnp.zeros((P, D), h.dtype).at[dest].set(xt[tok_sorted])
    block_expert = jnp.minimum(
        jnp.searchsorted(pend, jnp.arange(n_blocks) * DISPATCH_BLOCK, side='right'), N_EXPERTS - 1)

    def expert_block(args):
        xb, e = args
        hid = jax.nn.silu(xb @ w_gate[e]) * (xb @ w_up[e])
        return hid @ w_down[e]

    y_buf = lax.map(expert_block, (x_buf.reshape(n_blocks, DISPATCH_BLOCK, D), block_expert)).reshape(P, D)
    y = y_buf[dest] * w_sorted[:, None].astype(h.dtype)
    out = jnp.zeros((T, D), h.dtype).at[tok_sorted].add(y)
    return out.reshape(B, S, D)


def setup_inputs(seed: int = 0) -> dict:
    key = jax.random.key(seed)
    ks = jax.random.split(key, 22)
    L = DEPTH

    def nrm(k, shape, scale):
        return jax.random.normal(k, shape, jnp.float32) * scale

    return {
        'x': nrm(ks[0], (BATCH, SEQ, D_MODEL), 1.0),
        'c': nrm(ks[1], (BATCH, D_MODEL), 1.0),
        'ada_w': nrm(ks[2], (L, D_MODEL, N_MOD * D_MODEL), 0.5 * D_MODEL ** -0.5),
        'ada_b': nrm(ks[3], (L, N_MOD * D_MODEL), 0.01),
        'norm1_g': 1.0 + nrm(ks[4], (L, D_MODEL), 0.02),
        'norm2_g': 1.0 + nrm(ks[5], (L, D_MODEL), 0.02),
        'w_in': nrm(ks[6], (L, D_MODEL, PROJ_COLS), D_MODEL ** -0.5),
        'gmlp_ln_g': 1.0 + nrm(ks[7], (L, GMLP_WIDTH), 0.02),
        'gmlp_ln_b': nrm(ks[8], (L, GMLP_WIDTH), 0.02),
        'w_spatial': nrm(ks[9], (L, GMLP_GROUPS, GMLP_CHUNK, GMLP_CHUNK), GMLP_CHUNK ** -0.5),
        'b_spatial': 1.0 + nrm(ks[10], (L, GMLP_GROUPS, GMLP_CHUNK), 0.02),
        'w_branch_a': nrm(ks[11], (L, GMLP_WIDTH, D_MODEL), GMLP_WIDTH ** -0.5),
        'w_branch_b': nrm(ks[12], (L, ATT_WIDTH, D_MODEL), ATT_WIDTH ** -0.5),
        'w_out': nrm(ks[13], (L, D_MODEL, D_MODEL), D_MODEL ** -0.5),
        'w_router_group': nrm(ks[14], (L, D_MODEL, N_GROUPS), D_MODEL ** -0.5),
        'b_router_group': nrm(ks[15], (L, N_GROUPS), 0.01),
        'w_router_expert': nrm(ks[16], (L, D_MODEL, N_GROUPS, EXPERTS_PER_GROUP), D_MODEL ** -0.5),
        'b_router_expert': nrm(ks[17], (L, N_GROUPS, EXPERTS_PER_GROUP), 0.01),
        'w_gate': nrm(ks[18], (L, N_EXPERTS, D_MODEL, D_EXPERT), D_MODEL ** -0.5),
        'w_up': nrm(ks[19], (L, N_EXPERTS, D_MODEL, D_EXPERT), D_MODEL ** -0.5),
        'w_down': nrm(ks[20], (L, N_EXPERTS, D_EXPERT, D_MODEL), D_EXPERT ** -0.5),
        'final_norm_g': 1.0 + nrm(ks[21], (D_MODEL,), 0.02),
    }


def reference(x, c, ada_w, ada_b, norm1_g, norm2_g, w_in, gmlp_ln_g, gmlp_ln_b, w_spatial, b_spatial,
              w_branch_a, w_branch_b, w_out, w_router_group, b_router_group, w_router_expert,
              b_router_expert, w_gate, w_up, w_down, final_norm_g):
    B, S, _ = x.shape
    pos = jnp.arange(S)
    splits = [2 * GMLP_WIDTH, 2 * GMLP_WIDTH + ATT_WIDTH, 2 * GMLP_WIDTH + 2 * ATT_WIDTH,
              2 * GMLP_WIDTH + 3 * ATT_WIDTH, 2 * GMLP_WIDTH + 3 * ATT_WIDTH + D_MODEL]
    for l in range(DEPTH):
        mod = c @ ada_w[l] + ada_b[l]
        sh1, sc1, g1, sh2, sc2, g2 = jnp.split(mod, N_MOD, axis=-1)
        h = rmsnorm(x, norm1_g[l]) * (1.0 + sc1[:, None]) + sh1[:, None]
        proj = h @ w_in[l]
        p_a, q, k, v, gate_a, gate_b = jnp.split(proj, splits, axis=-1)
        y_a = spatial_gating_mixer(p_a, gmlp_ln_g[l], gmlp_ln_b[l], w_spatial[l], b_spatial[l]) @ w_branch_a[l]
        q = partial_rotary(q.reshape(B, S, ATT_HEADS, HEAD_DIM), pos)
        k = partial_rotary(k.reshape(B, S, ATT_HEADS, HEAD_DIM), pos)
        v = v.reshape(B, S, ATT_HEADS, HEAD_DIM)
        y_b = moba_attention(q, k, v) @ w_branch_b[l]
        merged = jax.nn.sigmoid(gate_a) * y_a + jax.nn.sigmoid(gate_b) * y_b
        x = x + g1[:, None] * (merged @ w_out[l])
        h = rmsnorm(x, norm2_g[l]) * (1.0 + sc2[:, None]) + sh2[:, None]
        x = x + g2[:, None] * hierarchical_moe(h, w_router_group[l], b_router_group[l], w_router_expert[l],
                                               b_router_expert[l], w_gate[l], w_up[l], w_down[l])
    return rmsnorm(x, final_norm_g)
```

```python
import functools

import jax
import jax.numpy as jnp
from jax import lax
from jax.experimental import pallas as pl
from jax.experimental.pallas import tpu as pltpu

D_MODEL = 1024
GMLP_WIDTH = D_MODEL // 2
GMLP_GROUPS = 4
GMLP_CH = GMLP_WIDTH // GMLP_GROUPS
GMLP_CHUNK = 128
ATT_HEADS = 8
HEAD_DIM = 64
ATT_WIDTH = ATT_HEADS * HEAD_DIM
ROPE_DIM = HEAD_DIM // 4
ROPE_THETA = 500000.0
MOBA_BLOCK = 256
MOBA_TOPK = 3
N_GROUPS = 4
EXPERTS_PER_GROUP = 8
N_EXPERTS = N_GROUPS * EXPERTS_PER_GROUP
TOP_K_EXPERT = 2
D_EXPERT = D_MODEL // 2
DISPATCH_BLOCK = 256
N_MOD = 6
EPS = 1e-6

LANES = 128
HEADS_PER_LANE_TILE = LANES // HEAD_DIM
VMEM_LIMIT_BYTES = 56 * 1024 * 1024

TOKEN_TILE = 512
ROW_DMA_TILE = 256

F32 = jnp.float32
BF16 = jnp.bfloat16
NEG_INF = float("-inf")


def _rms(x):
    return x * lax.rsqrt(jnp.mean(x * x, axis=-1, keepdims=True) + EPS)


def _first_index_of(mask, idx, sentinel):
    return jnp.min(jnp.where(mask, idx, sentinel), axis=1, keepdims=True)


def _ada_kernel(c_ref, w_ref, b_ref, o_ref):
    o_ref[...] = jnp.dot(c_ref[...], w_ref[...], preferred_element_type=F32,
                         precision=lax.Precision.HIGHEST) + b_ref[...]


def _ada_mod(c, ada_w, ada_b):
    B = c.shape[0]
    return pl.pallas_call(
        _ada_kernel,
        grid=(N_MOD,),
        in_specs=[pl.BlockSpec((B, D_MODEL), lambda j: (0, 0)),
                  pl.BlockSpec((D_MODEL, D_MODEL), lambda j: (0, j)),
                  pl.BlockSpec((1, D_MODEL), lambda j: (0, j))],
        out_specs=pl.BlockSpec((B, D_MODEL), lambda j: (0, j)),
        out_shape=jax.ShapeDtypeStruct((B, N_MOD * D_MODEL), F32),
        name="ada_mod",
    )(c, ada_w, ada_b)


def _mixer_in_kernel(x_ref, mod_ref, g1_ref, w_in_ref, wsp_ref, bsp_ref, lng_ref, lnb_ref, wba_ref,
                     cos_ref, sa_ref, sb_ref,
                     q_ref, k_ref, v_ref, a_ref, gb_ref, km_ref):
    tm = x_ref.shape[1]
    sh1 = mod_ref[0, 0:1, :]
    sc1 = mod_ref[0, 1:2, :]
    h = (_rms(x_ref[0]) * g1_ref[...]) * (1.0 + sc1) + sh1
    hb = h.astype(BF16)

    z = jax.nn.gelu(jnp.dot(hb, w_in_ref[:, 0:2 * GMLP_WIDTH], preferred_element_type=F32))
    u = z[:, :GMLP_WIDTH]
    v = z[:, GMLP_WIDTH:]
    mu = jnp.mean(v, axis=-1, keepdims=True)
    var = jnp.mean(jnp.square(v - mu), axis=-1, keepdims=True)
    vn = ((v - mu) * lax.rsqrt(var + EPS) * lng_ref[...] + lnb_ref[...]).astype(BF16)
    t_idx = lax.broadcasted_iota(jnp.int32, (GMLP_CHUNK, GMLP_CHUNK), 0)
    s_idx = lax.broadcasted_iota(jnp.int32, (GMLP_CHUNK, GMLP_CHUNK), 1)
    w_causal = [jnp.where(t_idx >= s_idx, wsp_ref[g], 0.0).astype(BF16) for g in range(GMLP_GROUPS)]
    chunks = []
    for c in range(tm // GMLP_CHUNK):
        rows = slice(c * GMLP_CHUNK, (c + 1) * GMLP_CHUNK)
        cols = [jnp.dot(w_causal[g], vn[rows, g * GMLP_CH:(g + 1) * GMLP_CH], preferred_element_type=F32)
                for g in range(GMLP_GROUPS)]
        chunks.append(jnp.concatenate(cols, axis=1) + bsp_ref[...])
    sv = jnp.concatenate(chunks, axis=0)
    ya = jnp.dot((u * sv).astype(BF16), wba_ref[...], preferred_element_type=F32)

    off = 2 * GMLP_WIDTH + 3 * ATT_WIDTH
    ga = jnp.dot(hb, w_in_ref[:, off:off + D_MODEL], preferred_element_type=F32)
    a_ref[0] = (jax.nn.sigmoid(ga) * ya).astype(BF16)
    gbv = jnp.dot(hb, w_in_ref[:, off + D_MODEL:off + 2 * D_MODEL], preferred_element_type=F32)
    gb_ref[0] = jax.nn.sigmoid(gbv).astype(BF16)

    qkv = jnp.dot(hb, w_in_ref[:, 2 * GMLP_WIDTH:2 * GMLP_WIDTH + 3 * ATT_WIDTH], preferred_element_type=F32)
    cos_t = cos_ref[...]
    sin_a = sa_ref[...]
    sin_b = sb_ref[...]
    half = ROPE_DIM // 2

    def rope(t):
        outs = []
        for j in range(ATT_WIDTH // LANES):
            tj = t[:, j * LANES:(j + 1) * LANES]
            outs.append(tj * cos_t + pltpu.roll(tj, LANES - half, 1) * sin_a + pltpu.roll(tj, half, 1) * sin_b)
        return jnp.concatenate(outs, axis=1)

    q = rope(qkv[:, :ATT_WIDTH]) * (HEAD_DIM ** -0.5)
    k = rope(qkv[:, ATT_WIDTH:2 * ATT_WIDTH])
    q_ref[0] = q.astype(BF16)
    k_ref[0] = k.astype(BF16)
    v_ref[0] = qkv[:, 2 * ATT_WIDTH:].astype(BF16)
    for r in range(tm // MOBA_BLOCK):
        km_ref[0, 0, r:r + 1, :] = jnp.mean(k[r * MOBA_BLOCK:(r + 1) * MOBA_BLOCK], axis=0, keepdims=True)


def _const_spec(shape):
    nd = len(shape)
    return pl.BlockSpec(shape, lambda *_: (0,) * nd, pipeline_mode=pl.Buffered(1))


def _mixer_in(x, mod3, g1, w_in, w_spatial, bias_sp, ln_g, ln_b, w_branch_a, cos_t, sin_a, sin_b):
    B, S, D = x.shape
    tm = TOKEN_TILE
    nt = S // tm
    tok = lambda w: pl.BlockSpec((1, tm, w), lambda b, i: (b, i, 0))
    tab = pl.BlockSpec((tm, LANES), lambda b, i: (i, 0))
    return pl.pallas_call(
        _mixer_in_kernel,
        grid=(B, nt),
        in_specs=[tok(D),
                  pl.BlockSpec((1, N_MOD, D), lambda b, i: (b, 0, 0)),
                  _const_spec((1, D)),
                  _const_spec(w_in.shape),
                  _const_spec(w_spatial.shape),
                  _const_spec(bias_sp.shape),
                  _const_spec((1, GMLP_WIDTH)),
                  _const_spec((1, GMLP_WIDTH)),
                  _const_spec(w_branch_a.shape),
                  tab, tab, tab],
        out_specs=[tok(ATT_WIDTH), tok(ATT_WIDTH), tok(ATT_WIDTH), tok(D), tok(D),
                   pl.BlockSpec((1, 1, tm // MOBA_BLOCK, ATT_WIDTH), lambda b, i: (b, i, 0, 0))],
        out_shape=[jax.ShapeDtypeStruct((B, S, ATT_WIDTH), BF16)] * 3
                  + [jax.ShapeDtypeStruct((B, S, D), BF16)] * 2
                  + [jax.ShapeDtypeStruct((B, nt, tm // MOBA_BLOCK, ATT_WIDTH), F32)],
        compiler_params=pltpu.CompilerParams(dimension_semantics=("arbitrary", "arbitrary"),
                                             vmem_limit_bytes=VMEM_LIMIT_BYTES),
        name="mixer_in",
    )(x, mod3, g1, w_in, w_spatial, bias_sp, ln_g, ln_b, w_branch_a, cos_t, sin_a, sin_b)


def _moba_kernel(q_ref, k_ref, v_ref, km_ref, o_ref, m_ref, l_ref, acc_ref, bits_ref):
    i = pl.program_id(2)
    tq = q_ref.shape[1]
    nb = km_ref.shape[1]
    q2 = q_ref[0]
    lane = lax.broadcasted_iota(jnp.int32, (tq, LANES), 1)
    zero = jnp.zeros_like(q2)
    qs = jnp.concatenate([jnp.where(lane < HEAD_DIM, q2, zero), jnp.where(lane >= HEAD_DIM, q2, zero)], axis=0)
    nt_dims = (((1,), (1,)), ((), ()))

    gate = lax.dot_general(qs.astype(F32), km_ref[0], nt_dims, preferred_element_type=F32,
                           precision=lax.Precision.HIGHEST)
    n_idx = lax.broadcasted_iota(jnp.int32, (2 * tq, nb), 1)
    g = jnp.where(n_idx < i, gate, NEG_INF)
    bits = jnp.zeros((2 * tq, 1), jnp.int32)
    for r in range(MOBA_TOPK):
        mx = jnp.max(g, axis=1, keepdims=True)
        first = _first_index_of(g == mx, n_idx, nb)
        bits = bits | jnp.where(r < i, jnp.left_shift(1, first), 0)
        g = jnp.where(n_idx == first, NEG_INF, g)
    bits_ref[...] = jnp.broadcast_to(bits, (2 * tq, LANES))

    own = pl.multiple_of(i * MOBA_BLOCK, MOBA_BLOCK)
    row = lax.broadcasted_iota(jnp.int32, (tq, MOBA_BLOCK), 0)
    col = lax.broadcasted_iota(jnp.int32, (tq, MOBA_BLOCK), 1)
    causal = jnp.concatenate([col <= row, col <= row], axis=0)
    s = lax.dot_general(qs, k_ref[0, pl.ds(own, MOBA_BLOCK), :], nt_dims, preferred_element_type=F32)
    s = jnp.where(causal, s, NEG_INF)
    m0 = jnp.max(s, axis=1, keepdims=True)
    p = jnp.exp(s - m0)
    m_ref[...] = jnp.broadcast_to(m0, (2 * tq, LANES))
    l_ref[...] = jnp.broadcast_to(jnp.sum(p, axis=1, keepdims=True), (2 * tq, LANES))
    acc_ref[...] = jnp.dot(p.astype(BF16), v_ref[0, pl.ds(own, MOBA_BLOCK), :], preferred_element_type=F32)

    def past_block(j, carry):
        start = pl.multiple_of(j * MOBA_BLOCK, MOBA_BLOCK)
        s = lax.dot_general(qs, k_ref[0, pl.ds(start, MOBA_BLOCK), :], nt_dims, preferred_element_type=F32)
        picked = (lax.shift_right_logical(bits_ref[...], j) & 1) == 1
        bias = jnp.where(picked, 0.0, NEG_INF)
        s = s + jnp.concatenate([bias, bias], axis=1)
        m_old = m_ref[...]
        m_new = jnp.maximum(m_old, jnp.max(s, axis=1, keepdims=True))
        alpha = jnp.exp(m_old - m_new)
        p = jnp.exp(s - jnp.concatenate([m_new, m_new], axis=1))
        l_ref[...] = alpha * l_ref[...] + jnp.sum(p, axis=1, keepdims=True)
        acc_ref[...] = alpha * acc_ref[...] + jnp.dot(p.astype(BF16), v_ref[0, pl.ds(start, MOBA_BLOCK), :],
                                                      preferred_element_type=F32)
        m_ref[...] = m_new
        return carry

    lax.fori_loop(0, i, past_block, 0)

    out = acc_ref[...] / l_ref[...]
    o_ref[0] = jnp.where(lane < HEAD_DIM, out[:tq], out[tq:]).astype(o_ref.dtype)


def _moba(q, k, v, kmean):
    B, S, _ = q.shape
    tq = MOBA_BLOCK
    nb = S // MOBA_BLOCK
    n_pairs = ATT_HEADS // HEADS_PER_LANE_TILE
    return pl.pallas_call(
        _moba_kernel,
        grid=(B, n_pairs, S // tq),
        in_specs=[pl.BlockSpec((1, tq, LANES), lambda b, hp, i: (b, i, hp)),
                  pl.BlockSpec((1, S, LANES), lambda b, hp, i: (b, 0, hp)),
                  pl.BlockSpec((1, S, LANES), lambda b, hp, i: (b, 0, hp)),
                  pl.BlockSpec((1, nb, LANES), lambda b, hp, i: (b, 0, hp))],
        out_specs=pl.BlockSpec((1, tq, LANES), lambda b, hp, i: (b, i, hp)),
        out_shape=jax.ShapeDtypeStruct((B, S, ATT_WIDTH), BF16),
        scratch_shapes=[pltpu.VMEM((2 * tq, LANES), F32),
                        pltpu.VMEM((2 * tq, LANES), F32),
                        pltpu.VMEM((2 * tq, LANES), F32),
                        pltpu.VMEM((2 * tq, LANES), jnp.int32)],
        compiler_params=pltpu.CompilerParams(dimension_semantics=("arbitrary", "arbitrary", "arbitrary"),
                                             vmem_limit_bytes=VMEM_LIMIT_BYTES),
        name="moba",
    )(q, k, v, kmean)


ROUTER_GROUP_LANE0 = N_EXPERTS


def _mixer_out_kernel(att_ref, a_ref, gb_ref, x_ref, mod_ref, wbb_ref, wout_ref, g2_ref, wr_ref, br_ref,
                      x1_ref, h2_ref, ri_ref, rw_ref, cnt_ref, run_ref):
    tm = x_ref.shape[1]

    @pl.when((pl.program_id(0) == 0) & (pl.program_id(1) == 0))
    def _():
        run_ref[...] = jnp.zeros_like(run_ref)

    yb = jnp.dot(att_ref[0], wbb_ref[...], preferred_element_type=F32)
    merged = a_ref[0].astype(F32) + gb_ref[0].astype(F32) * yb
    mo = jnp.dot(merged.astype(BF16), wout_ref[...], preferred_element_type=F32)
    g1 = mod_ref[0, 2:3, :]
    sh2 = mod_ref[0, 3:4, :]
    sc2 = mod_ref[0, 4:5, :]
    x1 = x_ref[0] + g1 * mo
    x1_ref[0] = x1
    h2 = (_rms(x1) * g2_ref[...]) * (1.0 + sc2) + sh2
    h2_ref[0] = h2

    logits = jnp.dot(h2, wr_ref[...], preferred_element_type=F32, precision=lax.Precision.HIGHEST) + br_ref[...]
    lane = lax.broadcasted_iota(jnp.int32, (tm, LANES), 1)
    is_group = (lane >= ROUTER_GROUP_LANE0) & (lane < ROUTER_GROUP_LANE0 + N_GROUPS)
    gl = jnp.where(is_group, logits, NEG_INF)
    ge = jnp.exp(gl - jnp.max(gl, axis=1, keepdims=True))
    g_prob = jnp.where(is_group, ge / jnp.sum(ge, axis=1, keepdims=True), -1.0)
    g_p = jnp.max(g_prob, axis=1, keepdims=True)
    g_idx = _first_index_of(g_prob == g_p, lane, LANES) - ROUTER_GROUP_LANE0
    in_group = (lane >= g_idx * EXPERTS_PER_GROUP) & (lane < (g_idx + 1) * EXPERTS_PER_GROUP)
    el = jnp.where(in_group, logits, NEG_INF)
    ee = jnp.exp(el - jnp.max(el, axis=1, keepdims=True))
    e_prob = jnp.where(in_group, ee / jnp.sum(ee, axis=1, keepdims=True), -1.0)
    p1 = jnp.max(e_prob, axis=1, keepdims=True)
    i1 = _first_index_of(e_prob == p1, lane, LANES)
    e_prob2 = jnp.where(lane == i1, -1.0, e_prob)
    p2 = jnp.max(e_prob2, axis=1, keepdims=True)
    i2 = _first_index_of(e_prob2 == p2, lane, LANES)
    w1 = g_p * p1 / (p1 + p2)
    w2 = g_p * p2 / (p1 + p2)

    hit1 = lane == i1
    hit2 = lane == i2
    onehot = jnp.where(hit1 | hit2, 1.0, 0.0)
    r_i = lax.broadcasted_iota(jnp.int32, (tm, tm), 0)
    c_i = lax.broadcasted_iota(jnp.int32, (tm, tm), 1)
    earlier = jnp.where(r_i > c_i, 1.0, 0.0).astype(BF16)
    cum = jnp.dot(earlier, onehot.astype(BF16), preferred_element_type=F32) + run_ref[...]
    rank1 = jnp.sum(jnp.where(hit1, cum, 0.0), axis=1, keepdims=True).astype(jnp.int32)
    rank2 = jnp.sum(jnp.where(hit2, cum, 0.0), axis=1, keepdims=True).astype(jnp.int32)
    run_ref[...] = run_ref[...] + jnp.sum(onehot, axis=0, keepdims=True)
    cnt_ref[...] = run_ref[...]

    ri_ref[0] = jnp.where(lane == 0, i1, jnp.where(lane == 1, i2,
                          jnp.where(lane == 2, rank1, jnp.where(lane == 3, rank2, 0))))
    rw_ref[0] = jnp.where(lane == 0, w1, jnp.where(lane == 1, w2, 0.0))


def _mixer_out(att, a_out, gb, x, mod3, w_branch_b, w_out, g2, w_router, b_router):
    B, S, D = x.shape
    tm = TOKEN_TILE
    tok = lambda w: pl.BlockSpec((1, tm, w), lambda b, i: (b, i, 0))
    return pl.pallas_call(
        _mixer_out_kernel,
        grid=(B, S // tm),
        in_specs=[tok(ATT_WIDTH), tok(D), tok(D), tok(D),
                  pl.BlockSpec((1, N_MOD, D), lambda b, i: (b, 0, 0)),
                  _const_spec(w_branch_b.shape),
                  _const_spec(w_out.shape),
                  _const_spec((1, D)),
                  _const_spec(w_router.shape),
                  _const_spec((1, LANES))],
        out_specs=[tok(D), tok(D), tok(LANES), tok(LANES),
                   pl.BlockSpec((1, LANES), lambda b, i: (0, 0))],
        out_shape=[jax.ShapeDtypeStruct((B, S, D), F32),
                   jax.ShapeDtypeStruct((B, S, D), F32),
                   jax.ShapeDtypeStruct((B, S, LANES), jnp.int32),
                   jax.ShapeDtypeStruct((B, S, LANES), F32),
                   jax.ShapeDtypeStruct((1, LANES), F32)],
        scratch_shapes=[pltpu.VMEM((1, LANES), F32)],
        compiler_params=pltpu.CompilerParams(dimension_semantics=("arbitrary", "arbitrary"),
                                             vmem_limit_bytes=VMEM_LIMIT_BYTES),
        name="mixer_out",
    )(att, a_out, gb, x, mod3, w_branch_b, w_out, g2, w_router, b_router)


def _row_copies(src_ref, dst_ref, src_rows, dst_rows, r, sem):
    return pltpu.make_async_copy(src_ref.at[pl.ds(src_rows(r), 1), :], dst_ref.at[pl.ds(dst_rows(r), 1), :], sem)


def _dispatch_kernel(d1_ref, d2_ref, h_ref, xb_in_ref, xb_ref, sem):
    del xb_in_ref
    n = h_ref.shape[0]
    same = lambda r: r
    copies = [functools.partial(_row_copies, h_ref, xb_ref, same, lambda r, d=d: d[0, 0, r])
              for d in (d1_ref, d2_ref)]

    def start(r, carry):
        for cp in copies:
            cp(r, sem).start()
        return carry

    def wait(r, carry):
        for cp in copies:
            cp(r, sem).wait()
        return carry

    lax.fori_loop(0, n, start, 0)
    lax.fori_loop(0, n, wait, 0)


def _dispatch(h2, dest1, dest2, n_rows):
    T, D = h2.shape
    n = ROW_DMA_TILE
    idx = pl.BlockSpec((1, 1, n), lambda i: (i, 0, 0), memory_space=pltpu.SMEM)
    return pl.pallas_call(
        _dispatch_kernel,
        grid=(T // n,),
        in_specs=[idx, idx,
                  pl.BlockSpec((n, D), lambda i: (i, 0)),
                  pl.BlockSpec(memory_space=pl.ANY)],
        out_specs=pl.BlockSpec(memory_space=pl.ANY),
        out_shape=jax.ShapeDtypeStruct((n_rows, D), F32),
        scratch_shapes=[pltpu.SemaphoreType.DMA(())],
        input_output_aliases={3: 0},
        compiler_params=pltpu.CompilerParams(dimension_semantics=("arbitrary",)),
        name="dispatch",
    )(dest1.reshape(T // n, 1, n), dest2.reshape(T // n, 1, n), h2, jnp.zeros((n_rows, D), F32))


def _expert_kernel(be_ref, nu_ref, x_ref, wg_ref, wu_ref, wd_ref, y_ref):
    del be_ref
    i = pl.program_id(0)

    @pl.when(i < nu_ref[0])
    def _():
        xb = x_ref[...].astype(BF16)
        gate = jnp.dot(xb, wg_ref[0], preferred_element_type=F32)
        up = jnp.dot(xb, wu_ref[0], preferred_element_type=F32)
        hid = (jax.nn.silu(gate) * up).astype(BF16)
        y_ref[...] = jnp.dot(hid, wd_ref[0], preferred_element_type=F32)

    @pl.when(i >= nu_ref[0])
    def _():
        y_ref[...] = jnp.zeros_like(y_ref)


def _experts(x_buf, block_expert, n_used, w_gate, w_up, w_down):
    P, D = x_buf.shape
    n_blocks = P // DISPATCH_BLOCK
    grid_spec = pltpu.PrefetchScalarGridSpec(
        num_scalar_prefetch=2,
        grid=(n_blocks,),
        in_specs=[pl.BlockSpec((DISPATCH_BLOCK, D), lambda i, be, nu: (i, 0)),
                  pl.BlockSpec((1, D, D_EXPERT), lambda i, be, nu: (be[i], 0, 0)),
                  pl.BlockSpec((1, D, D_EXPERT), lambda i, be, nu: (be[i], 0, 0)),
                  pl.BlockSpec((1, D_EXPERT, D), lambda i, be, nu: (be[i], 0, 0))],
        out_specs=pl.BlockSpec((DISPATCH_BLOCK, D), lambda i, be, nu: (i, 0)),
    )
    return pl.pallas_call(
        _expert_kernel,
        grid_spec=grid_spec,
        out_shape=jax.ShapeDtypeStruct((P, D), F32),
        compiler_params=pltpu.CompilerParams(dimension_semantics=("arbitrary",),
                                             vmem_limit_bytes=VMEM_LIMIT_BYTES),
        name="experts",
    )(block_expert, n_used, x_buf, w_gate, w_up, w_down)


def _combine_kernel(d1_ref, d2_ref, x1_ref, rw_ref, mod_ref, fg_ref, yb_ref, o_ref, y1_ref, y2_ref, sem):
    n = x1_ref.shape[1]
    same = lambda r: r
    copies = [functools.partial(_row_copies, yb_ref, y, lambda r, d=d: d[0, 0, r], same)
              for d, y in ((d1_ref, y1_ref), (d2_ref, y2_ref))]

    def start(r, carry):
        for cp in copies:
            cp(r, sem).start()
        return carry

    def wait(r, carry):
        for cp in copies:
            cp(r, sem).wait()
        return carry

    lax.fori_loop(0, n, start, 0)
    lax.fori_loop(0, n, wait, 0)

    rw = rw_ref[0]
    moe = rw[:, 0:1] * y1_ref[...] + rw[:, 1:2] * y2_ref[...]
    g2 = mod_ref[0, 5:6, :]
    o_ref[0] = _rms(x1_ref[0] + g2 * moe) * fg_ref[...]


def _combine(x1, route_w, mod3, final_g, y_buf, dest1, dest2):
    B, S, D = x1.shape
    n = ROW_DMA_TILE
    nt = S // n
    idx = pl.BlockSpec((1, 1, n), lambda b, i: (b * nt + i, 0, 0), memory_space=pltpu.SMEM)
    tok = lambda w: pl.BlockSpec((1, n, w), lambda b, i: (b, i, 0))
    return pl.pallas_call(
        _combine_kernel,
        grid=(B, nt),
        in_specs=[idx, idx, tok(D), tok(LANES),
                  pl.BlockSpec((1, N_MOD, D), lambda b, i: (b, 0, 0)),
                  pl.BlockSpec((1, D), lambda b, i: (0, 0)),
                  pl.BlockSpec(memory_space=pl.ANY)],
        out_specs=tok(D),
        out_shape=jax.ShapeDtypeStruct((B, S, D), F32),
        scratch_shapes=[pltpu.VMEM((n, D), F32), pltpu.VMEM((n, D), F32), pltpu.SemaphoreType.DMA(())],
        compiler_params=pltpu.CompilerParams(dimension_semantics=("arbitrary", "arbitrary")),
        name="combine",
    )(dest1.reshape(B * nt, 1, n), dest2.reshape(B * nt, 1, n), x1, route_w, mod3, final_g, y_buf)


def _rope_tables(S):
    half = ROPE_DIM // 2
    inv_freq = jnp.power(ROPE_THETA, -jnp.arange(half, dtype=jnp.float32) * 2.0 / ROPE_DIM)
    ang = jnp.arange(S).astype(jnp.float32)[:, None] * inv_freq[None, :]
    cos = jnp.cos(ang)
    sin = jnp.sin(ang)
    rest = HEAD_DIM - ROPE_DIM
    ones = jnp.ones((S, rest), F32)
    zeros = jnp.zeros((S, rest), F32)
    zh = jnp.zeros((S, half), F32)
    per_head = lambda parts: jnp.tile(jnp.concatenate(parts, axis=1), (1, HEADS_PER_LANE_TILE))
    return per_head([cos, cos, ones]), per_head([-sin, zh, zeros]), per_head([zh, sin, zeros])


def kernel(x, c, ada_w, ada_b, norm1_g, norm2_g, w_in, gmlp_ln_g, gmlp_ln_b, w_spatial, b_spatial, w_branch_a,
           w_branch_b, w_out, w_router_group, b_router_group, w_router_expert, b_router_expert, w_gate, w_up,
           w_down, final_norm_g):
    B, S, D = x.shape
    T = B * S
    assert D == D_MODEL and S % TOKEN_TILE == 0 and S % MOBA_BLOCK == 0 and S // MOBA_BLOCK <= 32
    assert ada_w.shape[0] == 1, "single layer"
    cos_t, sin_a, sin_b = _rope_tables(S)

    mod3 = _ada_mod(c, ada_w[0], ada_b).reshape(B, N_MOD, D)

    bias_sp = jnp.repeat(b_spatial[0].T, GMLP_CH, axis=1)
    q, k, v, a_out, gb, kmean = _mixer_in(
        x, mod3, norm1_g, w_in[0].astype(BF16), w_spatial[0], bias_sp, gmlp_ln_g, gmlp_ln_b,
        w_branch_a[0].astype(BF16), cos_t, sin_a, sin_b)
    att = _moba(q, k, v, kmean.reshape(B, S // MOBA_BLOCK, ATT_WIDTH))

    w_router = jnp.zeros((D, LANES), F32)
    w_router = w_router.at[:, :N_EXPERTS].set(w_router_expert[0].reshape(D, N_EXPERTS))
    w_router = w_router.at[:, ROUTER_GROUP_LANE0:ROUTER_GROUP_LANE0 + N_GROUPS].set(w_router_group[0])
    b_router = jnp.zeros((1, LANES), F32)
    b_router = b_router.at[0, :N_EXPERTS].set(b_router_expert[0].reshape(N_EXPERTS))
    b_router = b_router.at[0, ROUTER_GROUP_LANE0:ROUTER_GROUP_LANE0 + N_GROUPS].set(b_router_group[0])
    x1, h2, route_i, route_w, counts = _mixer_out(
        att, a_out, gb, x, mod3, w_branch_b[0].astype(BF16), w_out[0].astype(BF16), norm2_g, w_router, b_router)

    counts = counts[0, :N_EXPERTS].astype(jnp.int32)
    padded = (counts + DISPATCH_BLOCK - 1) // DISPATCH_BLOCK * DISPATCH_BLOCK
    pend = jnp.cumsum(padded)
    pstart = pend - padded
    n_blocks = -(-(T * TOP_K_EXPERT) // DISPATCH_BLOCK) + N_EXPERTS
    block_expert = jnp.minimum(
        jnp.searchsorted(pend, jnp.arange(n_blocks) * DISPATCH_BLOCK, side='right'), N_EXPERTS - 1).astype(jnp.int32)
    n_used = (pend[-1:] // DISPATCH_BLOCK).astype(jnp.int32)
    route_i = route_i.reshape(T, LANES)
    dest1 = pstart[route_i[:, 0]] + route_i[:, 2]
    dest2 = pstart[route_i[:, 1]] + route_i[:, 3]

    x_buf = _dispatch(h2.reshape(T, D), dest1, dest2, n_blocks * DISPATCH_BLOCK)
    y_buf = _experts(x_buf, block_expert, n_used, w_gate[0].astype(BF16), w_up[0].astype(BF16),
                     w_down[0].astype(BF16))
    return _combine(x1, route_w, mod3, final_norm_g.reshape(1, D), y_buf, dest1, dest2)
```

```python
import functools

import jax
import jax.numpy as jnp
from jax import lax
from jax.experimental import pallas as pl
from jax.experimental.pallas import tpu as pltpu

D_MODEL = 1024
GMLP_WIDTH = D_MODEL // 2
GMLP_GROUPS = 4
GMLP_CH = GMLP_WIDTH // GMLP_GROUPS
GMLP_CHUNK = 128
ATT_HEADS = 8
HEAD_DIM = 64
ATT_WIDTH = ATT_HEADS * HEAD_DIM
ROPE_DIM = HEAD_DIM // 4
ROPE_THETA = 500000.0
MOBA_BLOCK = 256
MOBA_TOPK = 3
N_GROUPS = 4
EXPERTS_PER_GROUP = 8
N_EXPERTS = N_GROUPS * EXPERTS_PER_GROUP
TOP_K_EXPERT = 2
D_EXPERT = D_MODEL // 2
DISPATCH_BLOCK = 256
N_MOD = 6
EPS = 1e-6

LANES = 128
HEADS_PER_LANE_TILE = LANES // HEAD_DIM
VMEM_LIMIT_BYTES = 56 * 1024 * 1024

TOKEN_TILE = 512
ROW_DMA_TILE = 256

F32 = jnp.float32
BF16 = jnp.bfloat16
NEG_INF = float("-inf")
LOG2_E = 1.4426950408889634


def _rms(x):
    return x * lax.rsqrt(jnp.mean(x * x, axis=-1, keepdims=True) + EPS)


def _first_index_of(mask, idx, sentinel):
    return jnp.min(jnp.where(mask, idx, sentinel), axis=1, keepdims=True)


def _ada_kernel(c_ref, w_ref, b_ref, o_ref):
    o_ref[...] = jnp.dot(c_ref[...], w_ref[...], preferred_element_type=F32,
                         precision=lax.Precision.HIGHEST) + b_ref[...]


def _ada_mod(c, ada_w, ada_b):
    B = c.shape[0]
    return pl.pallas_call(
        _ada_kernel,
        grid=(N_MOD,),
        in_specs=[pl.BlockSpec((B, D_MODEL), lambda j: (0, 0)),
                  pl.BlockSpec((D_MODEL, D_MODEL), lambda j: (0, j)),
                  pl.BlockSpec((1, D_MODEL), lambda j: (0, j))],
        out_specs=pl.BlockSpec((B, D_MODEL), lambda j: (0, j)),
        out_shape=jax.ShapeDtypeStruct((B, N_MOD * D_MODEL), F32),
        name="ada_mod",
    )(c, ada_w, ada_b)


def _mixer_in_kernel(x_ref, mod_ref, g1_ref, w_in_ref, wsp_ref, bsp_ref, lng_ref, lnb_ref, wba_ref,
                     cos_ref, sa_ref, sb_ref,
                     q_ref, k_ref, v_ref, a_ref, gb_ref, km_ref):
    tm = x_ref.shape[1]
    sh1 = mod_ref[0, 0:1, :]
    sc1 = mod_ref[0, 1:2, :]
    h = (_rms(x_ref[0]) * g1_ref[...]) * (1.0 + sc1) + sh1
    hb = h.astype(BF16)

    z = jax.nn.gelu(jnp.dot(hb, w_in_ref[:, 0:2 * GMLP_WIDTH], preferred_element_type=F32))
    u = z[:, :GMLP_WIDTH]
    v = z[:, GMLP_WIDTH:]
    mu = jnp.mean(v, axis=-1, keepdims=True)
    var = jnp.mean(jnp.square(v - mu), axis=-1, keepdims=True)
    vn = ((v - mu) * lax.rsqrt(var + EPS) * lng_ref[...] + lnb_ref[...]).astype(BF16)
    t_idx = lax.broadcasted_iota(jnp.int32, (GMLP_CHUNK, GMLP_CHUNK), 0)
    s_idx = lax.broadcasted_iota(jnp.int32, (GMLP_CHUNK, GMLP_CHUNK), 1)
    w_causal = [jnp.where(t_idx >= s_idx, wsp_ref[g], 0.0).astype(BF16) for g in range(GMLP_GROUPS)]
    chunks = []
    for c in range(tm // GMLP_CHUNK):
        rows = slice(c * GMLP_CHUNK, (c + 1) * GMLP_CHUNK)
        cols = [jnp.dot(w_causal[g], vn[rows, g * GMLP_CH:(g + 1) * GMLP_CH], preferred_element_type=F32)
                for g in range(GMLP_GROUPS)]
        chunks.append(jnp.concatenate(cols, axis=1) + bsp_ref[...])
    sv = jnp.concatenate(chunks, axis=0)
    ya = jnp.dot((u * sv).astype(BF16), wba_ref[...], preferred_element_type=F32)

    off = 2 * GMLP_WIDTH + 3 * ATT_WIDTH
    ga = jnp.dot(hb, w_in_ref[:, off:off + D_MODEL], preferred_element_type=F32)
    a_ref[0] = (jax.nn.sigmoid(ga) * ya).astype(BF16)
    gbv = jnp.dot(hb, w_in_ref[:, off + D_MODEL:off + 2 * D_MODEL], preferred_element_type=F32)
    gb_ref[0] = jax.nn.sigmoid(gbv).astype(BF16)

    qkv = jnp.dot(hb, w_in_ref[:, 2 * GMLP_WIDTH:2 * GMLP_WIDTH + 3 * ATT_WIDTH], preferred_element_type=F32)
    cos_t = cos_ref[...]
    sin_a = sa_ref[...]
    sin_b = sb_ref[...]
    half = ROPE_DIM // 2

    def rope(t):
        outs = []
        for j in range(ATT_WIDTH // LANES):
            tj = t[:, j * LANES:(j + 1) * LANES]
            outs.append(tj * cos_t + pltpu.roll(tj, LANES - half, 1) * sin_a + pltpu.roll(tj, half, 1) * sin_b)
        return jnp.concatenate(outs, axis=1)

    q = rope(qkv[:, :ATT_WIDTH]) * (HEAD_DIM ** -0.5 * LOG2_E)
    k = rope(qkv[:, ATT_WIDTH:2 * ATT_WIDTH])
    q_ref[0] = q.astype(BF16)
    k_ref[0] = k.astype(BF16)
    v_ref[0] = qkv[:, 2 * ATT_WIDTH:].astype(BF16)
    for r in range(tm // MOBA_BLOCK):
        km_ref[0, 0, r:r + 1, :] = jnp.mean(k[r * MOBA_BLOCK:(r + 1) * MOBA_BLOCK], axis=0, keepdims=True)


def _const_spec(shape):
    nd = len(shape)
    return pl.BlockSpec(shape, lambda *_: (0,) * nd, pipeline_mode=pl.Buffered(1))


def _mixer_in(x, mod3, g1, w_in, w_spatial, bias_sp, ln_g, ln_b, w_branch_a, cos_t, sin_a, sin_b):
    B, S, D = x.shape
    tm = TOKEN_TILE
    nt = S // tm
    tok = lambda w: pl.BlockSpec((1, tm, w), lambda b, i: (b, i, 0))
    tab = pl.BlockSpec((tm, LANES), lambda b, i: (i, 0))
    return pl.pallas_call(
        _mixer_in_kernel,
        grid=(B, nt),
        in_specs=[tok(D),
                  pl.BlockSpec((1, N_MOD, D), lambda b, i: (b, 0, 0)),
                  _const_spec((1, D)),
                  _const_spec(w_in.shape),
                  _const_spec(w_spatial.shape),
                  _const_spec(bias_sp.shape),
                  _const_spec((1, GMLP_WIDTH)),
                  _const_spec((1, GMLP_WIDTH)),
                  _const_spec(w_branch_a.shape),
                  tab, tab, tab],
        out_specs=[tok(ATT_WIDTH), tok(ATT_WIDTH), tok(ATT_WIDTH), tok(D), tok(D),
                   pl.BlockSpec((1, 1, tm // MOBA_BLOCK, ATT_WIDTH), lambda b, i: (b, i, 0, 0))],
        out_shape=[jax.ShapeDtypeStruct((B, S, ATT_WIDTH), BF16)] * 3
                  + [jax.ShapeDtypeStruct((B, S, D), BF16)] * 2
                  + [jax.ShapeDtypeStruct((B, nt, tm // MOBA_BLOCK, ATT_WIDTH), F32)],
        compiler_params=pltpu.CompilerParams(dimension_semantics=("arbitrary", "arbitrary"),
                                             vmem_limit_bytes=VMEM_LIMIT_BYTES),
        name="mixer_in",
    )(x, mod3, g1, w_in, w_spatial, bias_sp, ln_g, ln_b, w_branch_a, cos_t, sin_a, sin_b)


MOBA_ROW_CHUNK = 128


def _moba_kernel(q_ref, k_ref, v_ref, km_ref, o_ref, m_ref, acc_ref, bits_ref,
                 s0_ref, s1_ref, p0_ref, p1_ref, a0_ref, a1_ref):
    i = pl.program_id(2)
    tq = q_ref.shape[1]
    nb = km_ref.shape[1]
    n_rows = HEADS_PER_LANE_TILE * tq
    ch = MOBA_ROW_CHUNK
    q2 = q_ref[0]
    lane = lax.broadcasted_iota(jnp.int32, (tq, LANES), 1)
    zero = jnp.zeros_like(q2)
    qs = jnp.concatenate([jnp.where(lane < HEAD_DIM, q2, zero), jnp.where(lane >= HEAD_DIM, q2, zero)], axis=0)
    nt_dims = (((1,), (1,)), ((), ()))

    gate = lax.dot_general(km_ref[0], qs.astype(F32), nt_dims, preferred_element_type=F32,
                           precision=lax.Precision.HIGHEST)
    n_idx = lax.broadcasted_iota(jnp.int32, (nb, n_rows), 0)
    g = jnp.where(n_idx < i, gate, NEG_INF)
    bits = jnp.zeros((1, n_rows), jnp.int32)
    for r in range(MOBA_TOPK):
        mx = jnp.max(g, axis=0, keepdims=True)
        first = jnp.min(jnp.where(g == mx, n_idx, nb), axis=0, keepdims=True)
        bits = bits | jnp.where(r < i, jnp.left_shift(1, first), 0)
        g = jnp.where(n_idx == first, NEG_INF, g)
    for c in range(n_rows // LANES):
        blk = jnp.broadcast_to(bits[:, c * LANES:(c + 1) * LANES], (LANES, LANES))
        bits_ref[c * LANES:(c + 1) * LANES, :] = blk.T

    def block_start(t):
        return pl.multiple_of(jnp.where(t == 0, i, t - 1) * MOBA_BLOCK, MOBA_BLOCK)

    def scores(t, s_ref):
        t = jnp.minimum(t, i)
        s_ref[...] = lax.dot_general(qs, k_ref[0, pl.ds(block_start(t), MOBA_BLOCK), :], nt_dims,
                                     preferred_element_type=F32)

    def softmax_past(t, s_ref, p_ref, alpha_ref):
        j = jnp.minimum(t - 1, nb - 1)
        for c in range(n_rows // ch):
            rows = slice(c * ch, (c + 1) * ch)
            s = s_ref[rows, :]
            picked = (lax.shift_right_logical(bits_ref[rows, :], j) & 1) == 1
            m_old = m_ref[rows, :]
            m_new = jnp.where(picked, jnp.maximum(m_old, jnp.max(s, axis=1, keepdims=True)), m_old)
            alpha_ref[rows, :] = jnp.exp2(m_old - m_new)
            m_sub = jnp.where(picked, m_new, jnp.inf)
            p_ref[rows, :] = jnp.exp2(s - jnp.concatenate([m_sub, m_sub], axis=1)).astype(BF16)
            m_ref[rows, :] = m_new

    def accumulate(t, p_ref, alpha_ref):
        vj = v_ref[0, pl.ds(block_start(t), MOBA_BLOCK), :]
        lane_v = lax.broadcasted_iota(jnp.int32, vj.shape, 1)
        one = jnp.ones_like(vj)
        v_heads = (jnp.where(lane_v < HEAD_DIM, vj, one), jnp.where(lane_v < HEAD_DIM, one, vj))
        for h in range(HEADS_PER_LANE_TILE):
            rows = slice(h * tq, (h + 1) * tq)
            acc_ref[rows, :] = alpha_ref[rows, :] * acc_ref[rows, :] + jnp.dot(
                p_ref[rows, :], v_heads[h], preferred_element_type=F32)

    s_refs, p_refs, alpha_refs = (s0_ref, s1_ref), (p0_ref, p1_ref), (a0_ref, a1_ref)

    scores(0, s_refs[0])
    row = lax.broadcasted_iota(jnp.int32, (ch, MOBA_BLOCK), 0)
    col = lax.broadcasted_iota(jnp.int32, (ch, MOBA_BLOCK), 1)
    for c in range(n_rows // ch):
        rows = slice(c * ch, (c + 1) * ch)
        s = jnp.where(col <= row + (c * ch) % tq, s_refs[0][rows, :], NEG_INF)
        m0 = jnp.max(s, axis=1, keepdims=True)
        p_refs[0][rows, :] = jnp.exp2(s - m0).astype(BF16)
        m_ref[rows, :] = jnp.broadcast_to(m0, (ch, LANES))
    alpha_refs[0][...] = jnp.ones_like(alpha_refs[0])
    acc_ref[...] = jnp.zeros_like(acc_ref)
    scores(1, s_refs[1])

    def tick_pair(d, carry):
        for half in range(2):
            t = 2 * d + 1 + half
            cur, prev = (1 - half), half
            accumulate(t - 1, p_refs[prev], alpha_refs[prev])
            softmax_past(t, s_refs[cur], p_refs[cur], alpha_refs[cur])
            scores(t + 1, s_refs[prev])
        return carry

    lax.fori_loop(0, (i + 2) // 2, tick_pair, 0)

    acc = acc_ref[...]
    out = acc / pltpu.roll(acc, HEAD_DIM, 1)
    o_ref[0] = jnp.where(lane < HEAD_DIM, out[:tq], out[tq:]).astype(o_ref.dtype)


def _moba(q, k, v, kmean):
    B, S, _ = q.shape
    tq = MOBA_BLOCK
    nb = S // MOBA_BLOCK
    n_pairs = ATT_HEADS // HEADS_PER_LANE_TILE
    n_rows = HEADS_PER_LANE_TILE * tq
    return pl.pallas_call(
        _moba_kernel,
        grid=(B, n_pairs, S // tq),
        in_specs=[pl.BlockSpec((1, tq, LANES), lambda b, hp, i: (b, i, hp)),
                  pl.BlockSpec((1, S, LANES), lambda b, hp, i: (b, 0, hp)),
                  pl.BlockSpec((1, S, LANES), lambda b, hp, i: (b, 0, hp)),
                  pl.BlockSpec((1, nb, LANES), lambda b, hp, i: (b, 0, hp))],
        out_specs=pl.BlockSpec((1, tq, LANES), lambda b, hp, i: (b, i, hp)),
        out_shape=jax.ShapeDtypeStruct((B, S, ATT_WIDTH), BF16),
        scratch_shapes=[pltpu.VMEM((n_rows, LANES), F32),
                        pltpu.VMEM((n_rows, LANES), F32),
                        pltpu.VMEM((n_rows, LANES), jnp.int32),
                        pltpu.VMEM((n_rows, MOBA_BLOCK), F32),
                        pltpu.VMEM((n_rows, MOBA_BLOCK), F32),
                        pltpu.VMEM((n_rows, MOBA_BLOCK), BF16),
                        pltpu.VMEM((n_rows, MOBA_BLOCK), BF16),
                        pltpu.VMEM((n_rows, LANES), F32),
                        pltpu.VMEM((n_rows, LANES), F32)],
        compiler_params=pltpu.CompilerParams(dimension_semantics=("arbitrary", "arbitrary", "arbitrary"),
                                             vmem_limit_bytes=VMEM_LIMIT_BYTES),
        name="moba",
    )(q, k, v, kmean)


ROUTER_GROUP_LANE0 = N_EXPERTS


def _mixer_out_kernel(att_ref, a_ref, gb_ref, x_ref, mod_ref, wbb_ref, wout_ref, g2_ref, wr_ref, br_ref,
                      x1_ref, h2_ref, ri_ref, rw_ref, cnt_ref, run_ref):
    tm = x_ref.shape[1]

    @pl.when((pl.program_id(0) == 0) & (pl.program_id(1) == 0))
    def _():
        run_ref[...] = jnp.zeros_like(run_ref)

    yb = jnp.dot(att_ref[0], wbb_ref[...], preferred_element_type=F32)
    merged = a_ref[0].astype(F32) + gb_ref[0].astype(F32) * yb
    mo = jnp.dot(merged.astype(BF16), wout_ref[...], preferred_element_type=F32)
    g1 = mod_ref[0, 2:3, :]
    sh2 = mod_ref[0, 3:4, :]
    sc2 = mod_ref[0, 4:5, :]
    x1 = x_ref[0] + g1 * mo
    x1_ref[0] = x1
    h2 = (_rms(x1) * g2_ref[...]) * (1.0 + sc2) + sh2
    h2_ref[0] = h2

    logits = jnp.dot(h2, wr_ref[...], preferred_element_type=F32, precision=lax.Precision.HIGHEST) + br_ref[...]
    lane = lax.broadcasted_iota(jnp.int32, (tm, LANES), 1)
    is_group = (lane >= ROUTER_GROUP_LANE0) & (lane < ROUTER_GROUP_LANE0 + N_GROUPS)
    gl = jnp.where(is_group, logits, NEG_INF)
    ge = jnp.exp(gl - jnp.max(gl, axis=1, keepdims=True))
    g_prob = jnp.where(is_group, ge / jnp.sum(ge, axis=1, keepdims=True), -1.0)
    g_p = jnp.max(g_prob, axis=1, keepdims=True)
    g_idx = _first_index_of(g_prob == g_p, lane, LANES) - ROUTER_GROUP_LANE0
    in_group = (lane >= g_idx * EXPERTS_PER_GROUP) & (lane < (g_idx + 1) * EXPERTS_PER_GROUP)
    el = jnp.where(in_group, logits, NEG_INF)
    ee = jnp.exp(el - jnp.max(el, axis=1, keepdims=True))
    e_prob = jnp.where(in_group, ee / jnp.sum(ee, axis=1, keepdims=True), -1.0)
    p1 = jnp.max(e_prob, axis=1, keepdims=True)
    i1 = _first_index_of(e_prob == p1, lane, LANES)
    e_prob2 = jnp.where(lane == i1, -1.0, e_prob)
    p2 = jnp.max(e_prob2, axis=1, keepdims=True)
    i2 = _first_index_of(e_prob2 == p2, lane, LANES)
    w1 = g_p * p1 / (p1 + p2)
    w2 = g_p * p2 / (p1 + p2)

    hit1 = lane == i1
    hit2 = lane == i2
    onehot = jnp.where(hit1 | hit2, 1.0, 0.0)
    r_i = lax.broadcasted_iota(jnp.int32, (tm, tm), 0)
    c_i = lax.broadcasted_iota(jnp.int32, (tm, tm), 1)
    earlier = jnp.where(r_i > c_i, 1.0, 0.0).astype(BF16)
    cum = jnp.dot(earlier, onehot.astype(BF16), preferred_element_type=F32) + run_ref[...]
    rank1 = jnp.sum(jnp.where(hit1, cum, 0.0), axis=1, keepdims=True).astype(jnp.int32)
    rank2 = jnp.sum(jnp.where(hit2, cum, 0.0), axis=1, keepdims=True).astype(jnp.int32)
    run_ref[...] = run_ref[...] + jnp.sum(onehot, axis=0, keepdims=True)
    cnt_ref[...] = run_ref[...]

    ri_ref[0] = jnp.where(lane == 0, i1, jnp.where(lane == 1, i2,
                          jnp.where(lane == 2, rank1, jnp.where(lane == 3, rank2, 0))))
    rw_ref[0] = jnp.where(lane == 0, w1, jnp.where(lane == 1, w2, 0.0))


def _mixer_out(att, a_out, gb, x, mod3, w_branch_b, w_out, g2, w_router, b_router):
    B, S, D = x.shape
    tm = TOKEN_TILE
    tok = lambda w: pl.BlockSpec((1, tm, w), lambda b, i: (b, i, 0))
    return pl.pallas_call(
        _mixer_out_kernel,
        grid=(B, S // tm),
        in_specs=[tok(ATT_WIDTH), tok(D), tok(D), tok(D),
                  pl.BlockSpec((1, N_MOD, D), lambda b, i: (b, 0, 0)),
                  _const_spec(w_branch_b.shape),
                  _const_spec(w_out.shape),
                  _const_spec((1, D)),
                  _const_spec(w_router.shape),
                  _const_spec((1, LANES))],
        out_specs=[tok(D), tok(D), tok(LANES), tok(LANES),
                   pl.BlockSpec((1, LANES), lambda b, i: (0, 0))],
        out_shape=[jax.ShapeDtypeStruct((B, S, D), F32),
                   jax.ShapeDtypeStruct((B, S, D), F32),
                   jax.ShapeDtypeStruct((B, S, LANES), jnp.int32),
                   jax.ShapeDtypeStruct((B, S, LANES), F32),
                   jax.ShapeDtypeStruct((1, LANES), F32)],
        scratch_shapes=[pltpu.VMEM((1, LANES), F32)],
        compiler_params=pltpu.CompilerParams(dimension_semantics=("arbitrary", "arbitrary"),
                                             vmem_limit_bytes=VMEM_LIMIT_BYTES),
        name="mixer_out",
    )(att, a_out, gb, x, mod3, w_branch_b, w_out, g2, w_router, b_router)


def _row_copies(src_ref, dst_ref, src_rows, dst_rows, r, sem):
    return pltpu.make_async_copy(src_ref.at[pl.ds(src_rows(r), 1), :], dst_ref.at[pl.ds(dst_rows(r), 1), :], sem)


def _dispatch_kernel(d1_ref, d2_ref, h_ref, xb_in_ref, xb_ref, sem):
    del xb_in_ref
    n = h_ref.shape[0]
    same = lambda r: r
    copies = [functools.partial(_row_copies, h_ref, xb_ref, same, lambda r, d=d: d[0, 0, r])
              for d in (d1_ref, d2_ref)]

    def start(r, carry):
        for cp in copies:
            cp(r, sem).start()
        return carry

    def wait(r, carry):
        for cp in copies:
            cp(r, sem).wait()
        return carry

    lax.fori_loop(0, n, start, 0)
    lax.fori_loop(0, n, wait, 0)


def _dispatch(h2, dest1, dest2, n_rows):
    T, D = h2.shape
    n = ROW_DMA_TILE
    idx = pl.BlockSpec((1, 1, n), lambda i: (i, 0, 0), memory_space=pltpu.SMEM)
    return pl.pallas_call(
        _dispatch_kernel,
        grid=(T // n,),
        in_specs=[idx, idx,
                  pl.BlockSpec((n, D), lambda i: (i, 0)),
                  pl.BlockSpec(memory_space=pl.ANY)],
        out_specs=pl.BlockSpec(memory_space=pl.ANY),
        out_shape=jax.ShapeDtypeStruct((n_rows, D), F32),
        scratch_shapes=[pltpu.SemaphoreType.DMA(())],
        input_output_aliases={3: 0},
        compiler_params=pltpu.CompilerParams(dimension_semantics=("arbitrary",)),
        name="dispatch",
    )(dest1.reshape(T // n, 1, n), dest2.reshape(T // n, 1, n), h2, jnp.zeros((n_rows, D), F32))


def _expert_kernel(be_ref, nu_ref, x_ref, wg_ref, wu_ref, wd_ref, y_ref):
    del be_ref
    i = pl.program_id(0)

    @pl.when(i < nu_ref[0])
    def _():
        xb = x_ref[...].astype(BF16)
        gate = jnp.dot(xb, wg_ref[0], preferred_element_type=F32)
        up = jnp.dot(xb, wu_ref[0], preferred_element_type=F32)
        hid = (jax.nn.silu(gate) * up).astype(BF16)
        y_ref[...] = jnp.dot(hid, wd_ref[0], preferred_element_type=F32)

    @pl.when(i >= nu_ref[0])
    def _():
        y_ref[...] = jnp.zeros_like(y_ref)


def _experts(x_buf, block_expert, n_used, w_gate, w_up, w_down):
    P, D = x_buf.shape
    n_blocks = P // DISPATCH_BLOCK
    grid_spec = pltpu.PrefetchScalarGridSpec(
        num_scalar_prefetch=2,
        grid=(n_blocks,),
        in_specs=[pl.BlockSpec((DISPATCH_BLOCK, D), lambda i, be, nu: (i, 0)),
                  pl.BlockSpec((1, D, D_EXPERT), lambda i, be, nu: (be[i], 0, 0)),
                  pl.BlockSpec((1, D, D_EXPERT), lambda i, be, nu: (be[i], 0, 0)),
                  pl.BlockSpec((1, D_EXPERT, D), lambda i, be, nu: (be[i], 0, 0))],
        out_specs=pl.BlockSpec((DISPATCH_BLOCK, D), lambda i, be, nu: (i, 0)),
    )
    return pl.pallas_call(
        _expert_kernel,
        grid_spec=grid_spec,
        out_shape=jax.ShapeDtypeStruct((P, D), F32),
        compiler_params=pltpu.CompilerParams(dimension_semantics=("arbitrary",),
                                             vmem_limit_bytes=VMEM_LIMIT_BYTES),
        name="experts",
    )(block_expert, n_used, x_buf, w_gate, w_up, w_down)


def _combine_kernel(d1_ref, d2_ref, x1_ref, rw_ref, mod_ref, fg_ref, yb_ref, o_ref, y1_ref, y2_ref, sem):
    n = x1_ref.shape[1]
    same = lambda r: r
    copies = [functools.partial(_row_copies, yb_ref, y, lambda r, d=d: d[0, 0, r], same)
              for d, y in ((d1_ref, y1_ref), (d2_ref, y2_ref))]

    def start(r, carry):
        for cp in copies:
            cp(r, sem).start()
        return carry

    def wait(r, carry):
        for cp in copies:
            cp(r, sem).wait()
        return carry

    lax.fori_loop(0, n, start, 0)
    lax.fori_loop(0, n, wait, 0)

    rw = rw_ref[0]
    moe = rw[:, 0:1] * y1_ref[...] + rw[:, 1:2] * y2_ref[...]
    g2 = mod_ref[0, 5:6, :]
    o_ref[0] = _rms(x1_ref[0] + g2 * moe) * fg_ref[...]


def _combine(x1, route_w, mod3, final_g, y_buf, dest1, dest2):
    B, S, D = x1.shape
    n = ROW_DMA_TILE
    nt = S // n
    idx = pl.BlockSpec((1, 1, n), lambda b, i: (b * nt + i, 0, 0), memory_space=pltpu.SMEM)
    tok = lambda w: pl.BlockSpec((1, n, w), lambda b, i: (b, i, 0))
    return pl.pallas_call(
        _combine_kernel,
        grid=(B, nt),
        in_specs=[idx, idx, tok(D), tok(LANES),
                  pl.BlockSpec((1, N_MOD, D), lambda b, i: (b, 0, 0)),
                  pl.BlockSpec((1, D), lambda b, i: (0, 0)),
                  pl.BlockSpec(memory_space=pl.ANY)],
        out_specs=tok(D),
        out_shape=jax.ShapeDtypeStruct((B, S, D), F32),
        scratch_shapes=[pltpu.VMEM((n, D), F32), pltpu.VMEM((n, D), F32), pltpu.SemaphoreType.DMA(())],
        compiler_params=pltpu.CompilerParams(dimension_semantics=("arbitrary", "arbitrary")),
        name="combine",
    )(dest1.reshape(B * nt, 1, n), dest2.reshape(B * nt, 1, n), x1, route_w, mod3, final_g, y_buf)


def _rope_tables(S):
    half = ROPE_DIM // 2
    inv_freq = jnp.power(ROPE_THETA, -jnp.arange(half, dtype=jnp.float32) * 2.0 / ROPE_DIM)
    ang = jnp.arange(S).astype(jnp.float32)[:, None] * inv_freq[None, :]
    cos = jnp.cos(ang)
    sin = jnp.sin(ang)
    rest = HEAD_DIM - ROPE_DIM
    ones = jnp.ones((S, rest), F32)
    zeros = jnp.zeros((S, rest), F32)
    zh = jnp.zeros((S, half), F32)
    per_head = lambda parts: jnp.tile(jnp.concatenate(parts, axis=1), (1, HEADS_PER_LANE_TILE))
    return per_head([cos, cos, ones]), per_head([-sin, zh, zeros]), per_head([zh, sin, zeros])


def kernel(x, c, ada_w, ada_b, norm1_g, norm2_g, w_in, gmlp_ln_g, gmlp_ln_b, w_spatial, b_spatial, w_branch_a,
           w_branch_b, w_out, w_router_group, b_router_group, w_router_expert, b_router_expert, w_gate, w_up,
           w_down, final_norm_g):
    B, S, D = x.shape
    T = B * S
    assert D == D_MODEL and S % TOKEN_TILE == 0 and S % MOBA_BLOCK == 0 and S // MOBA_BLOCK <= 32
    assert ada_w.shape[0] == 1, "single layer"
    cos_t, sin_a, sin_b = _rope_tables(S)

    mod3 = _ada_mod(c, ada_w[0], ada_b).reshape(B, N_MOD, D)

    bias_sp = jnp.repeat(b_spatial[0].T, GMLP_CH, axis=1)
    q, k, v, a_out, gb, kmean = _mixer_in(
        x, mod3, norm1_g, w_in[0].astype(BF16), w_spatial[0], bias_sp, gmlp_ln_g, gmlp_ln_b,
        w_branch_a[0].astype(BF16), cos_t, sin_a, sin_b)
    att = _moba(q, k, v, kmean.reshape(B, S // MOBA_BLOCK, ATT_WIDTH))

    w_router = jnp.zeros((D, LANES), F32)
    w_router = w_router.at[:, :N_EXPERTS].set(w_router_expert[0].reshape(D, N_EXPERTS))
    w_router = w_router.at[:, ROUTER_GROUP_LANE0:ROUTER_GROUP_LANE0 + N_GROUPS].set(w_router_group[0])
    b_router = jnp.zeros((1, LANES), F32)
    b_router = b_router.at[0, :N_EXPERTS].set(b_router_expert[0].reshape(N_EXPERTS))
    b_router = b_router.at[0, ROUTER_GROUP_LANE0:ROUTER_GROUP_LANE0 + N_GROUPS].set(b_router_group[0])
    x1, h2, route_i, route_w, counts = _mixer_out(
        att, a_out, gb, x, mod3, w_branch_b[0].astype(BF16), w_out[0].astype(BF16), norm2_g, w_router, b_router)

    counts = counts[0, :N_EXPERTS].astype(jnp.int32)
    padded = (counts + DISPATCH_BLOCK - 1) // DISPATCH_BLOCK * DISPATCH_BLOCK
    pend = jnp.cumsum(padded)
    pstart = pend - padded
    n_blocks = -(-(T * TOP_K_EXPERT) // DISPATCH_BLOCK) + N_EXPERTS
    block_start = jnp.arange(n_blocks, dtype=jnp.int32) * DISPATCH_BLOCK
    block_expert = jnp.minimum(
        jnp.sum((pend[None, :] <= block_start[:, None]).astype(jnp.int32), axis=1), N_EXPERTS - 1)
    n_used = (pend[-1:] // DISPATCH_BLOCK).astype(jnp.int32)
    route_i = route_i.reshape(T, LANES)
    dest1 = pstart[route_i[:, 0]] + route_i[:, 2]
    dest2 = pstart[route_i[:, 1]] + route_i[:, 3]

    x_buf = _dispatch(h2.reshape(T, D), dest1, dest2, n_blocks * DISPATCH_BLOCK)
    y_buf = _experts(x_buf, block_expert, n_used, w_gate[0].astype(BF16), w_up[0].astype(BF16),
                     w_down[0].astype(BF16))
    return _combine(x1, route_w, mod3, final_norm_g.reshape(1, D), y_buf, dest1, dest2)
```

```python
import functools

import jax
import jax.numpy as jnp
from jax import lax
from jax.experimental import pallas as pl
from jax.experimental.pallas import tpu as pltpu

D_MODEL = 1024
GMLP_WIDTH = D_MODEL // 2
GMLP_GROUPS = 4
GMLP_CH = GMLP_WIDTH // GMLP_GROUPS
GMLP_CHUNK = 128
ATT_HEADS = 8
HEAD_DIM = 64
ATT_WIDTH = ATT_HEADS * HEAD_DIM
ROPE_DIM = HEAD_DIM // 4
ROPE_THETA = 500000.0
MOBA_BLOCK = 256
MOBA_TOPK = 3
N_GROUPS = 4
EXPERTS_PER_GROUP = 8
N_EXPERTS = N_GROUPS * EXPERTS_PER_GROUP
TOP_K_EXPERT = 2
D_EXPERT = D_MODEL // 2
DISPATCH_BLOCK = 256
N_MOD = 6
EPS = 1e-6

LANES = 128
HEADS_PER_LANE_TILE = LANES // HEAD_DIM
VMEM_LIMIT_BYTES = 56 * 1024 * 1024

TOKEN_TILE = 512
ROW_DMA_TILE = 256
ROW_DMA_UNROLL = 8

F32 = jnp.float32
BF16 = jnp.bfloat16
NEG_INF = float("-inf")
LOG2_E = 1.4426950408889634


def _rms(x):
    return x * lax.rsqrt(jnp.mean(x * x, axis=-1, keepdims=True) + EPS)


def _first_index_of(mask, idx, sentinel):
    return jnp.min(jnp.where(mask, idx, sentinel), axis=1, keepdims=True)


def _ada_kernel(c_ref, w_ref, b_ref, o_ref):
    o_ref[...] = jnp.dot(c_ref[...], w_ref[...], preferred_element_type=F32,
                         precision=lax.Precision.HIGHEST) + b_ref[...]


def _ada_mod(c, ada_w, ada_b):
    B = c.shape[0]
    return pl.pallas_call(
        _ada_kernel,
        grid=(N_MOD,),
        in_specs=[pl.BlockSpec((B, D_MODEL), lambda j: (0, 0)),
                  pl.BlockSpec((D_MODEL, D_MODEL), lambda j: (0, j)),
                  pl.BlockSpec((1, D_MODEL), lambda j: (0, j))],
        out_specs=pl.BlockSpec((B, D_MODEL), lambda j: (0, j)),
        out_shape=jax.ShapeDtypeStruct((B, N_MOD * D_MODEL), F32),
        name="ada_mod",
    )(c, ada_w, ada_b)


def _mixer_in_kernel(x_ref, mod_ref, g1_ref, w_in_ref, wsp_ref, bsp_ref, lng_ref, lnb_ref, wba_ref,
                     cos_ref, sa_ref, sb_ref,
                     q_ref, k_ref, v_ref, a_ref, gb_ref, km_ref):
    tm = x_ref.shape[1]
    sh1 = mod_ref[0, 0:1, :]
    sc1 = mod_ref[0, 1:2, :]
    h = (_rms(x_ref[0]) * g1_ref[...]) * (1.0 + sc1) + sh1
    hb = h.astype(BF16)

    z = jax.nn.gelu(jnp.dot(hb, w_in_ref[:, 0:2 * GMLP_WIDTH], preferred_element_type=F32))
    u = z[:, :GMLP_WIDTH]
    v = z[:, GMLP_WIDTH:]
    mu = jnp.mean(v, axis=-1, keepdims=True)
    var = jnp.mean(jnp.square(v - mu), axis=-1, keepdims=True)
    vn = ((v - mu) * lax.rsqrt(var + EPS) * lng_ref[...] + lnb_ref[...]).astype(BF16)
    t_idx = lax.broadcasted_iota(jnp.int32, (GMLP_CHUNK, GMLP_CHUNK), 0)
    s_idx = lax.broadcasted_iota(jnp.int32, (GMLP_CHUNK, GMLP_CHUNK), 1)
    w_causal = [jnp.where(t_idx >= s_idx, wsp_ref[g], 0.0).astype(BF16) for g in range(GMLP_GROUPS)]
    chunks = []
    for c in range(tm // GMLP_CHUNK):
        rows = slice(c * GMLP_CHUNK, (c + 1) * GMLP_CHUNK)
        cols = [jnp.dot(w_causal[g], vn[rows, g * GMLP_CH:(g + 1) * GMLP_CH], preferred_element_type=F32)
                for g in range(GMLP_GROUPS)]
        chunks.append(jnp.concatenate(cols, axis=1) + bsp_ref[...])
    sv = jnp.concatenate(chunks, axis=0)
    ya = jnp.dot((u * sv).astype(BF16), wba_ref[...], preferred_element_type=F32)

    off = 2 * GMLP_WIDTH + 3 * ATT_WIDTH
    ga = jnp.dot(hb, w_in_ref[:, off:off + D_MODEL], preferred_element_type=F32)
    a_ref[0] = (jax.nn.sigmoid(ga) * ya).astype(BF16)
    gbv = jnp.dot(hb, w_in_ref[:, off + D_MODEL:off + 2 * D_MODEL], preferred_element_type=F32)
    gb_ref[0] = jax.nn.sigmoid(gbv).astype(BF16)

    qkv = jnp.dot(hb, w_in_ref[:, 2 * GMLP_WIDTH:2 * GMLP_WIDTH + 3 * ATT_WIDTH], preferred_element_type=F32)
    cos_t = cos_ref[...]
    sin_a = sa_ref[...]
    sin_b = sb_ref[...]
    half = ROPE_DIM // 2

    def rope(t):
        outs = []
        for j in range(ATT_WIDTH // LANES):
            tj = t[:, j * LANES:(j + 1) * LANES]
            outs.append(tj * cos_t + pltpu.roll(tj, LANES - half, 1) * sin_a + pltpu.roll(tj, half, 1) * sin_b)
        return jnp.concatenate(outs, axis=1)

    q = rope(qkv[:, :ATT_WIDTH]) * (HEAD_DIM ** -0.5 * LOG2_E)
    k = rope(qkv[:, ATT_WIDTH:2 * ATT_WIDTH])
    q_ref[0] = q.astype(BF16)
    k_ref[0] = k.astype(BF16)
    v_ref[0] = qkv[:, 2 * ATT_WIDTH:].astype(BF16)
    for r in range(tm // MOBA_BLOCK):
        km_ref[0, 0, r:r + 1, :] = jnp.mean(k[r * MOBA_BLOCK:(r + 1) * MOBA_BLOCK], axis=0, keepdims=True)


def _const_spec(shape):
    nd = len(shape)
    return pl.BlockSpec(shape, lambda *_: (0,) * nd, pipeline_mode=pl.Buffered(1))


def _mixer_in(x, mod3, g1, w_in, w_spatial, bias_sp, ln_g, ln_b, w_branch_a, cos_t, sin_a, sin_b):
    B, S, D = x.shape
    tm = TOKEN_TILE
    nt = S // tm
    tok = lambda w: pl.BlockSpec((1, tm, w), lambda b, i: (b, i, 0))
    tab = pl.BlockSpec((tm, LANES), lambda b, i: (i, 0))
    return pl.pallas_call(
        _mixer_in_kernel,
        grid=(B, nt),
        in_specs=[tok(D),
                  pl.BlockSpec((1, N_MOD, D), lambda b, i: (b, 0, 0)),
                  _const_spec((1, D)),
                  _const_spec(w_in.shape),
                  _const_spec(w_spatial.shape),
                  _const_spec(bias_sp.shape),
                  _const_spec((1, GMLP_WIDTH)),
                  _const_spec((1, GMLP_WIDTH)),
                  _const_spec(w_branch_a.shape),
                  tab, tab, tab],
        out_specs=[tok(ATT_WIDTH), tok(ATT_WIDTH), tok(ATT_WIDTH), tok(D), tok(D),
                   pl.BlockSpec((1, 1, tm // MOBA_BLOCK, ATT_WIDTH), lambda b, i: (b, i, 0, 0))],
        out_shape=[jax.ShapeDtypeStruct((B, S, ATT_WIDTH), BF16)] * 3
                  + [jax.ShapeDtypeStruct((B, S, D), BF16)] * 2
                  + [jax.ShapeDtypeStruct((B, nt, tm // MOBA_BLOCK, ATT_WIDTH), F32)],
        compiler_params=pltpu.CompilerParams(dimension_semantics=("arbitrary", "arbitrary"),
                                             vmem_limit_bytes=VMEM_LIMIT_BYTES),
        name="mixer_in",
    )(x, mod3, g1, w_in, w_spatial, bias_sp, ln_g, ln_b, w_branch_a, cos_t, sin_a, sin_b)


MOBA_ROW_CHUNK = 128


def _moba_kernel(q_ref, k_ref, v_ref, km_ref, o_ref, m_ref, acc_ref, bits_ref,
                 s0_ref, s1_ref, p0_ref, p1_ref, a0_ref, a1_ref):
    i = pl.program_id(2)
    tq = q_ref.shape[1]
    nb = km_ref.shape[1]
    n_rows = HEADS_PER_LANE_TILE * tq
    ch = MOBA_ROW_CHUNK
    q2 = q_ref[0]
    lane = lax.broadcasted_iota(jnp.int32, (tq, LANES), 1)
    zero = jnp.zeros_like(q2)
    qs = jnp.concatenate([jnp.where(lane < HEAD_DIM, q2, zero), jnp.where(lane >= HEAD_DIM, q2, zero)], axis=0)
    nt_dims = (((1,), (1,)), ((), ()))

    gate = lax.dot_general(km_ref[0], qs.astype(F32), nt_dims, preferred_element_type=F32,
                           precision=lax.Precision.HIGHEST)
    n_idx = lax.broadcasted_iota(jnp.int32, (nb, n_rows), 0)
    g = jnp.where(n_idx < i, gate, NEG_INF)
    bits = jnp.zeros((1, n_rows), jnp.int32)
    for r in range(MOBA_TOPK):
        mx = jnp.max(g, axis=0, keepdims=True)
        first = jnp.min(jnp.where(g == mx, n_idx, nb), axis=0, keepdims=True)
        bits = bits | jnp.where(r < i, jnp.left_shift(1, first), 0)
        g = jnp.where(n_idx == first, NEG_INF, g)
    for c in range(n_rows // LANES):
        blk = jnp.broadcast_to(bits[:, c * LANES:(c + 1) * LANES], (LANES, LANES))
        bits_ref[c * LANES:(c + 1) * LANES, :] = blk.T

    def block_start(t):
        return pl.multiple_of(jnp.where(t == 0, i, t - 1) * MOBA_BLOCK, MOBA_BLOCK)

    def scores(t, s_ref):
        t = jnp.minimum(t, i)
        s_ref[...] = lax.dot_general(qs, k_ref[0, pl.ds(block_start(t), MOBA_BLOCK), :], nt_dims,
                                     preferred_element_type=F32)

    def softmax_past(t, s_ref, p_ref, alpha_ref):
        j = jnp.minimum(t - 1, nb - 1)
        for c in range(n_rows // ch):
            rows = slice(c * ch, (c + 1) * ch)
            s = s_ref[rows, :]
            picked = (lax.shift_right_logical(bits_ref[rows, :], j) & 1) == 1
            m_old = m_ref[rows, :]
            m_new = jnp.where(picked, jnp.maximum(m_old, jnp.max(s, axis=1, keepdims=True)), m_old)
            alpha_ref[rows, :] = jnp.exp2(m_old - m_new)
            m_sub = jnp.where(picked, m_new, jnp.inf)
            p_ref[rows, :] = jnp.exp2(s - jnp.concatenate([m_sub, m_sub], axis=1)).astype(BF16)
            m_ref[rows, :] = m_new

    def accumulate(t, p_ref, alpha_ref):
        vj = v_ref[0, pl.ds(block_start(t), MOBA_BLOCK), :]
        lane_v = lax.broadcasted_iota(jnp.int32, vj.shape, 1)
        one = jnp.ones_like(vj)
        v_heads = (jnp.where(lane_v < HEAD_DIM, vj, one), jnp.where(lane_v < HEAD_DIM, one, vj))
        for h in range(HEADS_PER_LANE_TILE):
            rows = slice(h * tq, (h + 1) * tq)
            acc_ref[rows, :] = alpha_ref[rows, :] * acc_ref[rows, :] + jnp.dot(
                p_ref[rows, :], v_heads[h], preferred_element_type=F32)

    s_refs, p_refs, alpha_refs = (s0_ref, s1_ref), (p0_ref, p1_ref), (a0_ref, a1_ref)

    scores(0, s_refs[0])
    row = lax.broadcasted_iota(jnp.int32, (ch, MOBA_BLOCK), 0)
    col = lax.broadcasted_iota(jnp.int32, (ch, MOBA_BLOCK), 1)
    for c in range(n_rows // ch):
        rows = slice(c * ch, (c + 1) * ch)
        s = jnp.where(col <= row + (c * ch) % tq, s_refs[0][rows, :], NEG_INF)
        m0 = jnp.max(s, axis=1, keepdims=True)
        p_refs[0][rows, :] = jnp.exp2(s - m0).astype(BF16)
        m_ref[rows, :] = jnp.broadcast_to(m0, (ch, LANES))
    alpha_refs[0][...] = jnp.ones_like(alpha_refs[0])
    acc_ref[...] = jnp.zeros_like(acc_ref)
    scores(1, s_refs[1])

    def tick_pair(d, carry):
        for half in range(2):
            t = 2 * d + 1 + half
            cur, prev = (1 - half), half
            accumulate(t - 1, p_refs[prev], alpha_refs[prev])
            softmax_past(t, s_refs[cur], p_refs[cur], alpha_refs[cur])
            scores(t + 1, s_refs[prev])
        return carry

    lax.fori_loop(0, (i + 2) // 2, tick_pair, 0)

    acc = acc_ref[...]
    out = acc / pltpu.roll(acc, HEAD_DIM, 1)
    o_ref[0] = jnp.where(lane < HEAD_DIM, out[:tq], out[tq:]).astype(o_ref.dtype)


def _moba(q, k, v, kmean):
    B, S, _ = q.shape
    tq = MOBA_BLOCK
    nb = S // MOBA_BLOCK
    n_pairs = ATT_HEADS // HEADS_PER_LANE_TILE
    n_rows = HEADS_PER_LANE_TILE * tq
    return pl.pallas_call(
        _moba_kernel,
        grid=(B, n_pairs, S // tq),
        in_specs=[pl.BlockSpec((1, tq, LANES), lambda b, hp, i: (b, i, hp)),
                  pl.BlockSpec((1, S, LANES), lambda b, hp, i: (b, 0, hp)),
                  pl.BlockSpec((1, S, LANES), lambda b, hp, i: (b, 0, hp)),
                  pl.BlockSpec((1, nb, LANES), lambda b, hp, i: (b, 0, hp))],
        out_specs=pl.BlockSpec((1, tq, LANES), lambda b, hp, i: (b, i, hp)),
        out_shape=jax.ShapeDtypeStruct((B, S, ATT_WIDTH), BF16),
        scratch_shapes=[pltpu.VMEM((n_rows, LANES), F32),
                        pltpu.VMEM((n_rows, LANES), F32),
                        pltpu.VMEM((n_rows, LANES), jnp.int32),
                        pltpu.VMEM((n_rows, MOBA_BLOCK), F32),
                        pltpu.VMEM((n_rows, MOBA_BLOCK), F32),
                        pltpu.VMEM((n_rows, MOBA_BLOCK), BF16),
                        pltpu.VMEM((n_rows, MOBA_BLOCK), BF16),
                        pltpu.VMEM((n_rows, LANES), F32),
                        pltpu.VMEM((n_rows, LANES), F32)],
        compiler_params=pltpu.CompilerParams(dimension_semantics=("arbitrary", "arbitrary", "arbitrary"),
                                             vmem_limit_bytes=VMEM_LIMIT_BYTES),
        name="moba",
    )(q, k, v, kmean)


ROUTER_GROUP_LANE0 = N_EXPERTS


def _mixer_out_kernel(att_ref, a_ref, gb_ref, x_ref, mod_ref, wbb_ref, wout_ref, g2_ref, wr_ref, br_ref,
                      x1_ref, h2_ref, ri_ref, rw_ref, cnt_ref, run_ref):
    tm = x_ref.shape[1]

    @pl.when((pl.program_id(0) == 0) & (pl.program_id(1) == 0))
    def _():
        run_ref[...] = jnp.zeros_like(run_ref)

    yb = jnp.dot(att_ref[0], wbb_ref[...], preferred_element_type=F32)
    merged = a_ref[0].astype(F32) + gb_ref[0].astype(F32) * yb
    mo = jnp.dot(merged.astype(BF16), wout_ref[...], preferred_element_type=F32)
    g1 = mod_ref[0, 2:3, :]
    sh2 = mod_ref[0, 3:4, :]
    sc2 = mod_ref[0, 4:5, :]
    x1 = x_ref[0] + g1 * mo
    x1_ref[0] = x1
    h2 = (_rms(x1) * g2_ref[...]) * (1.0 + sc2) + sh2
    h2_ref[0] = h2

    h_hi = h2.astype(BF16)
    h_lo = (h2 - h_hi.astype(F32)).astype(BF16)
    parts = jnp.dot(jnp.concatenate([h_hi, h_lo], axis=0), wr_ref[...], preferred_element_type=F32)
    logits = (parts[:tm, :LANES] + parts[tm:, :LANES] + parts[:tm, LANES:]) + br_ref[...]
    lane = lax.broadcasted_iota(jnp.int32, (tm, LANES), 1)
    is_group = (lane >= ROUTER_GROUP_LANE0) & (lane < ROUTER_GROUP_LANE0 + N_GROUPS)
    gl = jnp.where(is_group, logits, NEG_INF)
    ge = jnp.exp(gl - jnp.max(gl, axis=1, keepdims=True))
    g_prob = jnp.where(is_group, ge / jnp.sum(ge, axis=1, keepdims=True), -1.0)
    g_p = jnp.max(g_prob, axis=1, keepdims=True)
    lane_f = lane.astype(F32)
    g_idx = _first_index_of(g_prob == g_p, lane_f, float(LANES)) - float(ROUTER_GROUP_LANE0)
    in_group = (lane_f >= g_idx * EXPERTS_PER_GROUP) & (lane_f < (g_idx + 1.0) * EXPERTS_PER_GROUP)
    el = jnp.where(in_group, logits, NEG_INF)
    ee = jnp.exp(el - jnp.max(el, axis=1, keepdims=True))
    e_prob = jnp.where(in_group, ee / jnp.sum(ee, axis=1, keepdims=True), -1.0)
    p1 = jnp.max(e_prob, axis=1, keepdims=True)
    i1 = _first_index_of(e_prob == p1, lane_f, float(LANES))
    e_prob2 = jnp.where(lane_f == i1, -1.0, e_prob)
    p2 = jnp.max(e_prob2, axis=1, keepdims=True)
    i2 = _first_index_of(e_prob2 == p2, lane_f, float(LANES))
    w1 = g_p * p1 / (p1 + p2)
    w2 = g_p * p2 / (p1 + p2)

    hit1 = lane_f == i1
    hit2 = lane_f == i2
    onehot = jnp.where(hit1 | hit2, 1.0, 0.0)
    r_i = lax.broadcasted_iota(jnp.int32, (tm, tm), 0)
    c_i = lax.broadcasted_iota(jnp.int32, (tm, tm), 1)
    earlier = jnp.where(r_i > c_i, 1.0, 0.0).astype(BF16)
    cum = jnp.dot(earlier, onehot.astype(BF16), preferred_element_type=F32) + run_ref[...]
    rank1 = jnp.sum(jnp.where(hit1, cum, 0.0), axis=1, keepdims=True).astype(jnp.int32)
    rank2 = jnp.sum(jnp.where(hit2, cum, 0.0), axis=1, keepdims=True).astype(jnp.int32)
    run_ref[...] = run_ref[...] + jnp.sum(onehot, axis=0, keepdims=True)
    cnt_ref[...] = run_ref[...]

    ri_ref[0] = jnp.where(lane == 0, i1.astype(jnp.int32), jnp.where(lane == 1, i2.astype(jnp.int32),
                          jnp.where(lane == 2, rank1, jnp.where(lane == 3, rank2, 0))))
    rw_ref[0] = jnp.where(lane == 0, w1, jnp.where(lane == 1, w2, 0.0))


def _mixer_out(att, a_out, gb, x, mod3, w_branch_b, w_out, g2, w_router, b_router):
    B, S, D = x.shape
    tm = TOKEN_TILE
    tok = lambda w: pl.BlockSpec((1, tm, w), lambda b, i: (b, i, 0))
    return pl.pallas_call(
        _mixer_out_kernel,
        grid=(B, S // tm),
        in_specs=[tok(ATT_WIDTH), tok(D), tok(D), tok(D),
                  pl.BlockSpec((1, N_MOD, D), lambda b, i: (b, 0, 0)),
                  _const_spec(w_branch_b.shape),
                  _const_spec(w_out.shape),
                  _const_spec((1, D)),
                  _const_spec(w_router.shape),
                  _const_spec((1, LANES))],
        out_specs=[tok(D), tok(D), tok(LANES), tok(LANES),
                   pl.BlockSpec((1, LANES), lambda b, i: (0, 0))],
        out_shape=[jax.ShapeDtypeStruct((B, S, D), F32),
                   jax.ShapeDtypeStruct((B, S, D), F32),
                   jax.ShapeDtypeStruct((B, S, LANES), jnp.int32),
                   jax.ShapeDtypeStruct((B, S, LANES), F32),
                   jax.ShapeDtypeStruct((1, LANES), F32)],
        scratch_shapes=[pltpu.VMEM((1, LANES), F32)],
        compiler_params=pltpu.CompilerParams(dimension_semantics=("arbitrary", "arbitrary"),
                                             vmem_limit_bytes=VMEM_LIMIT_BYTES),
        name="mixer_out",
    )(att, a_out, gb, x, mod3, w_branch_b, w_out, g2, w_router, b_router)


def _row_copies(src_ref, dst_ref, src_rows, dst_rows, r, sem):
    return pltpu.make_async_copy(src_ref.at[pl.ds(src_rows(r), 1), :], dst_ref.at[pl.ds(dst_rows(r), 1), :], sem)


def _dispatch_kernel(d1_ref, d2_ref, h_ref, xb_in_ref, xb_ref, sem):
    del xb_in_ref
    n = h_ref.shape[0]
    same = lambda r: r
    copies = [functools.partial(_row_copies, h_ref, xb_ref, same, lambda r, d=d: d[0, 0, r])
              for d in (d1_ref, d2_ref)]

    def start(r, carry):
        for cp in copies:
            cp(r, sem).start()
        return carry

    lax.fori_loop(0, n, start, 0, unroll=ROW_DMA_UNROLL)
    for _ in copies:
        pltpu.make_async_copy(h_ref, xb_ref.at[pl.ds(0, n), :], sem).wait()


def _dispatch(h2, dest1, dest2, n_rows):
    T, D = h2.shape
    n = ROW_DMA_TILE
    idx = pl.BlockSpec((1, 1, n), lambda i: (i, 0, 0), memory_space=pltpu.SMEM)
    return pl.pallas_call(
        _dispatch_kernel,
        grid=(T // n,),
        in_specs=[idx, idx,
                  pl.BlockSpec((n, D), lambda i: (i, 0)),
                  pl.BlockSpec(memory_space=pl.ANY)],
        out_specs=pl.BlockSpec(memory_space=pl.ANY),
        out_shape=jax.ShapeDtypeStruct((n_rows, D), F32),
        scratch_shapes=[pltpu.SemaphoreType.DMA(())],
        input_output_aliases={3: 0},
        compiler_params=pltpu.CompilerParams(dimension_semantics=("arbitrary",)),
        name="dispatch",
    )(dest1.reshape(T // n, 1, n), dest2.reshape(T // n, 1, n), h2, jnp.zeros((n_rows, D), F32))


def _expert_kernel(be_ref, nu_ref, x_ref, wg_ref, wu_ref, wd_ref, y_ref, wg_bf, wu_bf, wd_bf):
    i = pl.program_id(0)
    used = i < nu_ref[0]

    @pl.when(used & ((i == 0) | (be_ref[i] != be_ref[jnp.maximum(i - 1, 0)])))
    def _():
        wg_bf[...] = wg_ref[0].astype(BF16)
        wu_bf[...] = wu_ref[0].astype(BF16)
        wd_bf[...] = wd_ref[0].astype(BF16)

    @pl.when(used)
    def _():
        xb = x_ref[...].astype(BF16)
        gate = jnp.dot(xb, wg_bf[...], preferred_element_type=F32)
        up = jnp.dot(xb, wu_bf[...], preferred_element_type=F32)
        hid = (jax.nn.silu(gate) * up).astype(BF16)
        y_ref[...] = jnp.dot(hid, wd_bf[...], preferred_element_type=F32)

    @pl.when(i >= nu_ref[0])
    def _():
        y_ref[...] = jnp.zeros_like(y_ref)


def _experts(x_buf, block_expert, n_used, w_gate, w_up, w_down):
    P, D = x_buf.shape
    n_blocks = P // DISPATCH_BLOCK
    grid_spec = pltpu.PrefetchScalarGridSpec(
        num_scalar_prefetch=2,
        grid=(n_blocks,),
        in_specs=[pl.BlockSpec((DISPATCH_BLOCK, D), lambda i, be, nu: (i, 0)),
                  pl.BlockSpec((1, D, D_EXPERT), lambda i, be, nu: (be[i], 0, 0)),
                  pl.BlockSpec((1, D, D_EXPERT), lambda i, be, nu: (be[i], 0, 0)),
                  pl.BlockSpec((1, D_EXPERT, D), lambda i, be, nu: (be[i], 0, 0))],
        out_specs=pl.BlockSpec((DISPATCH_BLOCK, D), lambda i, be, nu: (i, 0)),
        scratch_shapes=[pltpu.VMEM((D, D_EXPERT), BF16), pltpu.VMEM((D, D_EXPERT), BF16),
                        pltpu.VMEM((D_EXPERT, D), BF16)],
    )
    return pl.pallas_call(
        _expert_kernel,
        grid_spec=grid_spec,
        out_shape=jax.ShapeDtypeStruct((P, D), F32),
        compiler_params=pltpu.CompilerParams(dimension_semantics=("arbitrary",),
                                             vmem_limit_bytes=VMEM_LIMIT_BYTES),
        name="experts",
    )(block_expert, n_used, x_buf, w_gate, w_up, w_down)


def _combine_kernel(d1_ref, d2_ref, d1n_ref, d2n_ref, x1_ref, rw_ref, mod_ref, fg_ref, yb_ref, o_ref, y_ref, sem):
    n = x1_ref.shape[1]
    step = pl.program_id(0) * pl.num_programs(1) + pl.program_id(1)
    last = pl.num_programs(0) * pl.num_programs(1) - 1
    slot = step % 2

    def gather(dests, buf):
        def start(r, carry):
            for k, d in enumerate(dests):
                pltpu.make_async_copy(yb_ref.at[pl.ds(d[0, 0, r], 1), :], y_ref.at[buf, k, pl.ds(r, 1), :],
                                      sem.at[buf]).start()
            return carry
        lax.fori_loop(0, n, start, 0, unroll=ROW_DMA_UNROLL)

    @pl.when(step == 0)
    def _():
        gather((d1_ref, d2_ref), 0)

    @pl.when(step < last)
    def _():
        gather((d1n_ref, d2n_ref), 1 - slot)

    for k in range(TOP_K_EXPERT):
        pltpu.make_async_copy(yb_ref.at[pl.ds(0, n), :], y_ref.at[slot, k], sem.at[slot]).wait()

    rw = rw_ref[0]
    moe = rw[:, 0:1] * y_ref[slot, 0] + rw[:, 1:2] * y_ref[slot, 1]
    g2 = mod_ref[0, 5:6, :]
    o_ref[0] = _rms(x1_ref[0] + g2 * moe) * fg_ref[...]


def _combine(x1, route_w, mod3, final_g, y_buf, dest1, dest2):
    B, S, D = x1.shape
    n = ROW_DMA_TILE
    nt = S // n
    idx = pl.BlockSpec((1, 1, n), lambda b, i: (b * nt + i, 0, 0), memory_space=pltpu.SMEM)
    idx_next = pl.BlockSpec((1, 1, n), lambda b, i: (jnp.minimum(b * nt + i + 1, B * nt - 1), 0, 0),
                            memory_space=pltpu.SMEM)
    tok = lambda w: pl.BlockSpec((1, n, w), lambda b, i: (b, i, 0))
    d1 = dest1.reshape(B * nt, 1, n)
    d2 = dest2.reshape(B * nt, 1, n)
    return pl.pallas_call(
        _combine_kernel,
        grid=(B, nt),
        in_specs=[idx, idx, idx_next, idx_next, tok(D), tok(LANES),
                  pl.BlockSpec((1, N_MOD, D), lambda b, i: (b, 0, 0)),
                  pl.BlockSpec((1, D), lambda b, i: (0, 0)),
                  pl.BlockSpec(memory_space=pl.ANY)],
        out_specs=tok(D),
        out_shape=jax.ShapeDtypeStruct((B, S, D), F32),
        scratch_shapes=[pltpu.VMEM((2, TOP_K_EXPERT, n, D), F32), pltpu.SemaphoreType.DMA((2,))],
        compiler_params=pltpu.CompilerParams(dimension_semantics=("arbitrary", "arbitrary")),
        name="combine",
    )(d1, d2, d1, d2, x1, route_w, mod3, final_g, y_buf)


def _rope_tables(S):
    half = ROPE_DIM // 2
    inv_freq = jnp.power(ROPE_THETA, -jnp.arange(half, dtype=jnp.float32) * 2.0 / ROPE_DIM)
    ang = jnp.arange(S).astype(jnp.float32)[:, None] * inv_freq[None, :]
    cos = jnp.cos(ang)
    sin = jnp.sin(ang)
    rest = HEAD_DIM - ROPE_DIM
    ones = jnp.ones((S, rest), F32)
    zeros = jnp.zeros((S, rest), F32)
    zh = jnp.zeros((S, half), F32)
    per_head = lambda parts: jnp.tile(jnp.concatenate(parts, axis=1), (1, HEADS_PER_LANE_TILE))
    return per_head([cos, cos, ones]), per_head([-sin, zh, zeros]), per_head([zh, sin, zeros])


def kernel(x, c, ada_w, ada_b, norm1_g, norm2_g, w_in, gmlp_ln_g, gmlp_ln_b, w_spatial, b_spatial, w_branch_a,
           w_branch_b, w_out, w_router_group, b_router_group, w_router_expert, b_router_expert, w_gate, w_up,
           w_down, final_norm_g):
    B, S, D = x.shape
    T = B * S
    assert D == D_MODEL and S % TOKEN_TILE == 0 and S % MOBA_BLOCK == 0 and S // MOBA_BLOCK <= 32
    assert ada_w.shape[0] == 1, "single layer"
    cos_t, sin_a, sin_b = _rope_tables(S)

    mod3 = _ada_mod(c, ada_w[0], ada_b).reshape(B, N_MOD, D)

    bias_sp = jnp.repeat(b_spatial[0].T, GMLP_CH, axis=1)
    q, k, v, a_out, gb, kmean = _mixer_in(
        x, mod3, norm1_g, w_in[0].astype(BF16), w_spatial[0], bias_sp, gmlp_ln_g, gmlp_ln_b,
        w_branch_a[0].astype(BF16), cos_t, sin_a, sin_b)
    att = _moba(q, k, v, kmean.reshape(B, S // MOBA_BLOCK, ATT_WIDTH))

    w_router = jnp.zeros((D, LANES), F32)
    w_router = w_router.at[:, :N_EXPERTS].set(w_router_expert[0].reshape(D, N_EXPERTS))
    w_router = w_router.at[:, ROUTER_GROUP_LANE0:ROUTER_GROUP_LANE0 + N_GROUPS].set(w_router_group[0])
    w_router_hi = w_router.astype(BF16)
    w_router = jnp.concatenate([w_router_hi, (w_router - w_router_hi.astype(F32)).astype(BF16)], axis=1)
    b_router = jnp.zeros((1, LANES), F32)
    b_router = b_router.at[0, :N_EXPERTS].set(b_router_expert[0].reshape(N_EXPERTS))
    b_router = b_router.at[0, ROUTER_GROUP_LANE0:ROUTER_GROUP_LANE0 + N_GROUPS].set(b_router_group[0])
    x1, h2, route_i, route_w, counts = _mixer_out(
        att, a_out, gb, x, mod3, w_branch_b[0].astype(BF16), w_out[0].astype(BF16), norm2_g, w_router, b_router)

    counts = counts[0, :N_EXPERTS].astype(jnp.int32)
    padded = (counts + DISPATCH_BLOCK - 1) // DISPATCH_BLOCK * DISPATCH_BLOCK
    pend = jnp.cumsum(padded)
    pstart = pend - padded
    n_blocks = -(-(T * TOP_K_EXPERT) // DISPATCH_BLOCK) + N_EXPERTS
    block_start = jnp.arange(n_blocks, dtype=jnp.int32) * DISPATCH_BLOCK
    block_expert = jnp.minimum(
        jnp.sum((pend[None, :] <= block_start[:, None]).astype(jnp.int32), axis=1), N_EXPERTS - 1)
    n_used = (pend[-1:] // DISPATCH_BLOCK).astype(jnp.int32)
    route_i = route_i.reshape(T, LANES)
    dest1 = pstart[route_i[:, 0]] + route_i[:, 2]
    dest2 = pstart[route_i[:, 1]] + route_i[:, 3]

    x_buf = _dispatch(h2.reshape(T, D), dest1, dest2, n_blocks * DISPATCH_BLOCK)
    y_buf = _experts(x_buf, block_expert, n_used, w_gate[0], w_up[0], w_down[0])
    return _combine(x1, route_w, mod3, final_norm_g.reshape(1, D), y_buf, dest1, dest2)
```

```python
import functools

import jax
import jax.numpy as jnp
from jax import lax
from jax.experimental import pallas as pl
from jax.experimental.pallas import tpu as pltpu

D_MODEL = 1024
GMLP_WIDTH = D_MODEL // 2
GMLP_GROUPS = 4
GMLP_CH = GMLP_WIDTH // GMLP_GROUPS
GMLP_CHUNK = 128
ATT_HEADS = 8
HEAD_DIM = 64
ATT_WIDTH = ATT_HEADS * HEAD_DIM
ROPE_DIM = HEAD_DIM // 4
ROPE_THETA = 500000.0
MOBA_BLOCK = 256
MOBA_TOPK = 3
N_GROUPS = 4
EXPERTS_PER_GROUP = 8
N_EXPERTS = N_GROUPS * EXPERTS_PER_GROUP
TOP_K_EXPERT = 2
D_EXPERT = D_MODEL // 2
DISPATCH_BLOCK = 256
N_MOD = 6
EPS = 1e-6

LANES = 128
HEADS_PER_LANE_TILE = LANES // HEAD_DIM
VMEM_LIMIT_BYTES = 56 * 1024 * 1024

TOKEN_TILE = 512
ROW_DMA_TILE = 256
ROW_DMA_UNROLL = 8

F32 = jnp.float32
BF16 = jnp.bfloat16
NEG_INF = float("-inf")
LOG2_E = 1.4426950408889634


def _rms(x):
    return x * lax.rsqrt(jnp.mean(x * x, axis=-1, keepdims=True) + EPS)


def _first_index_of(mask, idx, sentinel):
    return jnp.min(jnp.where(mask, idx, sentinel), axis=1, keepdims=True)


def _ada_kernel(c_ref, w_ref, b_ref, o_ref):
    o_ref[...] = jnp.dot(c_ref[...], w_ref[...], preferred_element_type=F32,
                         precision=lax.Precision.HIGHEST) + b_ref[...]


def _ada_mod(c, ada_w, ada_b):
    B = c.shape[0]
    return pl.pallas_call(
        _ada_kernel,
        grid=(N_MOD,),
        in_specs=[pl.BlockSpec((B, D_MODEL), lambda j: (0, 0)),
                  pl.BlockSpec((D_MODEL, D_MODEL), lambda j: (0, j)),
                  pl.BlockSpec((1, D_MODEL), lambda j: (0, j))],
        out_specs=pl.BlockSpec((B, D_MODEL), lambda j: (0, j)),
        out_shape=jax.ShapeDtypeStruct((B, N_MOD * D_MODEL), F32),
        name="ada_mod",
    )(c, ada_w, ada_b)


def _mixer_in_kernel(x_ref, mod_ref, g1_ref, w_in_ref, wsp_ref, bsp_ref, lng_ref, lnb_ref, wba_ref,
                     cos_ref, sa_ref, sb_ref,
                     q_ref, k_ref, v_ref, a_ref, gb_ref, km_ref):
    tm = x_ref.shape[1]
    sh1 = mod_ref[0, 0:1, :]
    sc1 = mod_ref[0, 1:2, :]
    h = (_rms(x_ref[0]) * g1_ref[...]) * (1.0 + sc1) + sh1
    hb = h.astype(BF16)

    z = jax.nn.gelu(jnp.dot(hb, w_in_ref[:, 0:2 * GMLP_WIDTH], preferred_element_type=F32))
    u = z[:, :GMLP_WIDTH]
    v = z[:, GMLP_WIDTH:]
    mu = jnp.mean(v, axis=-1, keepdims=True)
    var = jnp.mean(jnp.square(v - mu), axis=-1, keepdims=True)
    vn = ((v - mu) * lax.rsqrt(var + EPS) * lng_ref[...] + lnb_ref[...]).astype(BF16)
    t_idx = lax.broadcasted_iota(jnp.int32, (GMLP_CHUNK, GMLP_CHUNK), 0)
    s_idx = lax.broadcasted_iota(jnp.int32, (GMLP_CHUNK, GMLP_CHUNK), 1)
    w_causal = [jnp.where(t_idx >= s_idx, wsp_ref[g], 0.0).astype(BF16) for g in range(GMLP_GROUPS)]
    chunks = []
    for c in range(tm // GMLP_CHUNK):
        rows = slice(c * GMLP_CHUNK, (c + 1) * GMLP_CHUNK)
        cols = [jnp.dot(w_causal[g], vn[rows, g * GMLP_CH:(g + 1) * GMLP_CH], preferred_element_type=F32)
                for g in range(GMLP_GROUPS)]
        chunks.append(jnp.concatenate(cols, axis=1) + bsp_ref[...])
    sv = jnp.concatenate(chunks, axis=0)
    ya = jnp.dot((u * sv).astype(BF16), wba_ref[...], preferred_element_type=F32)

    off = 2 * GMLP_WIDTH + 3 * ATT_WIDTH
    ga = jnp.dot(hb, w_in_ref[:, off:off + D_MODEL], preferred_element_type=F32)
    a_ref[0] = (jax.nn.sigmoid(ga) * ya).astype(BF16)
    gbv = jnp.dot(hb, w_in_ref[:, off + D_MODEL:off + 2 * D_MODEL], preferred_element_type=F32)
    gb_ref[0] = jax.nn.sigmoid(gbv).astype(BF16)

    qkv = jnp.dot(hb, w_in_ref[:, 2 * GMLP_WIDTH:2 * GMLP_WIDTH + 3 * ATT_WIDTH], preferred_element_type=F32)
    cos_t = cos_ref[...]
    sin_a = sa_ref[...]
    sin_b = sb_ref[...]
    half = ROPE_DIM // 2

    def rope(t):
        outs = []
        for j in range(ATT_WIDTH // LANES):
            tj = t[:, j * LANES:(j + 1) * LANES]
            outs.append(tj * cos_t + pltpu.roll(tj, LANES - half, 1) * sin_a + pltpu.roll(tj, half, 1) * sin_b)
        return jnp.concatenate(outs, axis=1)

    q = rope(qkv[:, :ATT_WIDTH]) * (HEAD_DIM ** -0.5 * LOG2_E)
    k = rope(qkv[:, ATT_WIDTH:2 * ATT_WIDTH])
    q_ref[0] = q.astype(BF16)
    k_ref[0] = k.astype(BF16)
    v_ref[0] = qkv[:, 2 * ATT_WIDTH:].T.astype(BF16)
    for r in range(tm // MOBA_BLOCK):
        km_ref[0, 0, r:r + 1, :] = jnp.mean(k[r * MOBA_BLOCK:(r + 1) * MOBA_BLOCK], axis=0, keepdims=True)


def _const_spec(shape):
    nd = len(shape)
    return pl.BlockSpec(shape, lambda *_: (0,) * nd, pipeline_mode=pl.Buffered(1))


def _mixer_in(x, mod3, g1, w_in, w_spatial, bias_sp, ln_g, ln_b, w_branch_a, cos_t, sin_a, sin_b):
    B, S, D = x.shape
    tm = TOKEN_TILE
    nt = S // tm
    tok = lambda w: pl.BlockSpec((1, tm, w), lambda b, i: (b, i, 0))
    tab = pl.BlockSpec((tm, LANES), lambda b, i: (i, 0))
    return pl.pallas_call(
        _mixer_in_kernel,
        grid=(B, nt),
        in_specs=[tok(D),
                  pl.BlockSpec((1, N_MOD, D), lambda b, i: (b, 0, 0)),
                  _const_spec((1, D)),
                  _const_spec(w_in.shape),
                  _const_spec(w_spatial.shape),
                  _const_spec(bias_sp.shape),
                  _const_spec((1, GMLP_WIDTH)),
                  _const_spec((1, GMLP_WIDTH)),
                  _const_spec(w_branch_a.shape),
                  tab, tab, tab],
        out_specs=[tok(ATT_WIDTH), tok(ATT_WIDTH),
                   pl.BlockSpec((1, ATT_WIDTH, tm), lambda b, i: (b, 0, i)), tok(D), tok(D),
                   pl.BlockSpec((1, 1, tm // MOBA_BLOCK, ATT_WIDTH), lambda b, i: (b, i, 0, 0))],
        out_shape=[jax.ShapeDtypeStruct((B, S, ATT_WIDTH), BF16)] * 2
                  + [jax.ShapeDtypeStruct((B, ATT_WIDTH, S), BF16)]
                  + [jax.ShapeDtypeStruct((B, S, D), BF16)] * 2
                  + [jax.ShapeDtypeStruct((B, nt, tm // MOBA_BLOCK, ATT_WIDTH), F32)],
        compiler_params=pltpu.CompilerParams(dimension_semantics=("arbitrary", "arbitrary"),
                                             vmem_limit_bytes=VMEM_LIMIT_BYTES),
        name="mixer_in",
    )(x, mod3, g1, w_in, w_spatial, bias_sp, ln_g, ln_b, w_branch_a, cos_t, sin_a, sin_b)


def _moba_kernel(q_ref, k_ref, vt_ref, km_ref, o_ref, m_ref, acc_ref, bits_ref,
                 s0_ref, s1_ref, p0_ref, p1_ref, a0_ref, a1_ref):
    i = pl.program_id(2)
    tq = q_ref.shape[1]
    nb = km_ref.shape[1]
    n_cols = HEADS_PER_LANE_TILE * tq
    q2 = q_ref[0]
    lane = lax.broadcasted_iota(jnp.int32, (tq, LANES), 1)
    zero = jnp.zeros_like(q2)
    qs = jnp.concatenate([jnp.where(lane < HEAD_DIM, q2, zero), jnp.where(lane >= HEAD_DIM, q2, zero)], axis=0)
    nt_dims = (((1,), (1,)), ((), ()))

    gate = lax.dot_general(km_ref[0], qs.astype(F32), nt_dims, preferred_element_type=F32,
                           precision=lax.Precision.HIGHEST)
    n_idx = lax.broadcasted_iota(jnp.int32, (nb, n_cols), 0)
    g = jnp.where(n_idx < i, gate, NEG_INF)
    bits = jnp.zeros((1, n_cols), jnp.int32)
    for r in range(MOBA_TOPK):
        mx = jnp.max(g, axis=0, keepdims=True)
        first = jnp.min(jnp.where(g == mx, n_idx, nb), axis=0, keepdims=True)
        bits = bits | jnp.where(r < i, jnp.left_shift(1, first), 0)
        g = jnp.where(n_idx == first, NEG_INF, g)
    bits_ref[...] = bits

    def block_start(t):
        return pl.multiple_of(jnp.where(t == 0, i, t - 1) * MOBA_BLOCK, MOBA_BLOCK)

    def scores(t, s_ref):
        t = jnp.minimum(t, i)
        s_ref[...] = lax.dot_general(k_ref[0, pl.ds(block_start(t), MOBA_BLOCK), :], qs, nt_dims,
                                     preferred_element_type=F32)

    def softmax_past(t, s_ref, p_ref, alpha_ref):
        j = jnp.minimum(t - 1, nb - 1)
        for c in range(n_cols // LANES):
            cols = slice(c * LANES, (c + 1) * LANES)
            s = s_ref[:, cols]
            picked = (lax.shift_right_logical(bits_ref[:, cols], j) & 1) == 1
            m_old = m_ref[:, cols]
            m_new = jnp.where(picked, jnp.maximum(m_old, jnp.max(s, axis=0, keepdims=True)), m_old)
            alpha_ref[:, cols] = jnp.exp2(m_old - m_new)
            m_sub = jnp.where(picked, m_new, jnp.inf)
            p_ref[:, cols] = jnp.exp2(s - m_sub).astype(BF16)
            m_ref[:, cols] = m_new

    def accumulate(t, p_ref, alpha_ref):
        vt = vt_ref[0, :, pl.ds(block_start(t), MOBA_BLOCK)]
        feat = lax.broadcasted_iota(jnp.int32, vt.shape, 0)
        one = jnp.ones_like(vt)
        v_heads = (jnp.where(feat < HEAD_DIM, vt, one), jnp.where(feat < HEAD_DIM, one, vt))
        for h in range(HEADS_PER_LANE_TILE):
            cols = slice(h * tq, (h + 1) * tq)
            acc_ref[:, cols] = alpha_ref[:, cols] * acc_ref[:, cols] + jnp.dot(
                v_heads[h], p_ref[:, cols], preferred_element_type=F32)

    s_refs, p_refs, alpha_refs = (s0_ref, s1_ref), (p0_ref, p1_ref), (a0_ref, a1_ref)

    scores(0, s_refs[0])
    key = lax.broadcasted_iota(jnp.int32, (MOBA_BLOCK, LANES), 0)
    qry = lax.broadcasted_iota(jnp.int32, (MOBA_BLOCK, LANES), 1)
    for c in range(n_cols // LANES):
        cols = slice(c * LANES, (c + 1) * LANES)
        s = jnp.where(key <= qry + (c * LANES) % tq, s_refs[0][:, cols], NEG_INF)
        m0 = jnp.max(s, axis=0, keepdims=True)
        p_refs[0][:, cols] = jnp.exp2(s - m0).astype(BF16)
        m_ref[:, cols] = m0
    alpha_refs[0][...] = jnp.ones_like(alpha_refs[0])
    acc_ref[...] = jnp.zeros_like(acc_ref)
    scores(1, s_refs[1])

    def tick_pair(d, carry):
        for half in range(2):
            t = 2 * d + 1 + half
            cur, prev = (1 - half), half
            accumulate(t - 1, p_refs[prev], alpha_refs[prev])
            softmax_past(t, s_refs[cur], p_refs[cur], alpha_refs[cur])
            scores(t + 1, s_refs[prev])
        return carry

    lax.fori_loop(0, (i + 2) // 2, tick_pair, 0)

    acc = acc_ref[...]
    out0 = acc[:HEAD_DIM, :tq] / acc[HEAD_DIM:, :tq]
    out1 = acc[HEAD_DIM:, tq:] / acc[:HEAD_DIM, tq:]
    o_ref[0] = jnp.concatenate([out0, out1], axis=0).T.astype(o_ref.dtype)


def _moba(q, k, vt, kmean):
    B, S, _ = q.shape
    tq = MOBA_BLOCK
    nb = S // MOBA_BLOCK
    n_pairs = ATT_HEADS // HEADS_PER_LANE_TILE
    n_cols = HEADS_PER_LANE_TILE * tq
    return pl.pallas_call(
        _moba_kernel,
        grid=(B, n_pairs, S // tq),
        in_specs=[pl.BlockSpec((1, tq, LANES), lambda b, hp, i: (b, i, hp)),
                  pl.BlockSpec((1, S, LANES), lambda b, hp, i: (b, 0, hp)),
                  pl.BlockSpec((1, LANES, S), lambda b, hp, i: (b, hp, 0)),
                  pl.BlockSpec((1, nb, LANES), lambda b, hp, i: (b, 0, hp))],
        out_specs=pl.BlockSpec((1, tq, LANES), lambda b, hp, i: (b, i, hp)),
        out_shape=jax.ShapeDtypeStruct((B, S, ATT_WIDTH), BF16),
        scratch_shapes=[pltpu.VMEM((1, n_cols), F32),
                        pltpu.VMEM((LANES, n_cols), F32),
                        pltpu.VMEM((1, n_cols), jnp.int32),
                        pltpu.VMEM((MOBA_BLOCK, n_cols), F32),
                        pltpu.VMEM((MOBA_BLOCK, n_cols), F32),
                        pltpu.VMEM((MOBA_BLOCK, n_cols), BF16),
                        pltpu.VMEM((MOBA_BLOCK, n_cols), BF16),
                        pltpu.VMEM((1, n_cols), F32),
                        pltpu.VMEM((1, n_cols), F32)],
        compiler_params=pltpu.CompilerParams(dimension_semantics=("arbitrary", "arbitrary", "arbitrary"),
                                             vmem_limit_bytes=VMEM_LIMIT_BYTES),
        name="moba",
    )(q, k, vt, kmean)


ROUTER_GROUP_LANE0 = N_EXPERTS


def _mixer_out_kernel(att_ref, a_ref, gb_ref, x_ref, mod_ref, wbb_ref, wout_ref, g2_ref, wr_ref, br_ref,
                      x1_ref, h2_ref, ri_ref, rw_ref, cnt_ref, run_ref):
    tm = x_ref.shape[1]

    @pl.when((pl.program_id(0) == 0) & (pl.program_id(1) == 0))
    def _():
        run_ref[...] = jnp.zeros_like(run_ref)

    yb = jnp.dot(att_ref[0], wbb_ref[...], preferred_element_type=F32)
    merged = a_ref[0].astype(F32) + gb_ref[0].astype(F32) * yb
    mo = jnp.dot(merged.astype(BF16), wout_ref[...], preferred_element_type=F32)
    g1 = mod_ref[0, 2:3, :]
    sh2 = mod_ref[0, 3:4, :]
    sc2 = mod_ref[0, 4:5, :]
    x1 = x_ref[0] + g1 * mo
    x1_ref[0] = x1
    h2 = (_rms(x1) * g2_ref[...]) * (1.0 + sc2) + sh2
    h2_ref[0] = h2

    h_hi = h2.astype(BF16)
    h_lo = (h2 - h_hi.astype(F32)).astype(BF16)
    parts = jnp.dot(jnp.concatenate([h_hi, h_lo], axis=0), wr_ref[...], preferred_element_type=F32)
    logits = (parts[:tm, :LANES] + parts[tm:, :LANES] + parts[:tm, LANES:]) + br_ref[...]
    lane = lax.broadcasted_iota(jnp.int32, (tm, LANES), 1)
    is_group = (lane >= ROUTER_GROUP_LANE0) & (lane < ROUTER_GROUP_LANE0 + N_GROUPS)
    gl = jnp.where(is_group, logits, NEG_INF)
    ge = jnp.exp(gl - jnp.max(gl, axis=1, keepdims=True))
    g_prob = jnp.where(is_group, ge / jnp.sum(ge, axis=1, keepdims=True), -1.0)
    g_p = jnp.max(g_prob, axis=1, keepdims=True)
    lane_f = lane.astype(F32)
    g_idx = _first_index_of(g_prob == g_p, lane_f, float(LANES)) - float(ROUTER_GROUP_LANE0)
    in_group = (lane_f >= g_idx * EXPERTS_PER_GROUP) & (lane_f < (g_idx + 1.0) * EXPERTS_PER_GROUP)
    el = jnp.where(in_group, logits, NEG_INF)
    ee = jnp.exp(el - jnp.max(el, axis=1, keepdims=True))
    e_prob = jnp.where(in_group, ee / jnp.sum(ee, axis=1, keepdims=True), -1.0)
    p1 = jnp.max(e_prob, axis=1, keepdims=True)
    i1 = _first_index_of(e_prob == p1, lane_f, float(LANES))
    e_prob2 = jnp.where(lane_f == i1, -1.0, e_prob)
    p2 = jnp.max(e_prob2, axis=1, keepdims=True)
    i2 = _first_index_of(e_prob2 == p2, lane_f, float(LANES))
    w1 = g_p * p1 / (p1 + p2)
    w2 = g_p * p2 / (p1 + p2)

    hit1 = lane_f == i1
    hit2 = lane_f == i2
    onehot = jnp.where(hit1 | hit2, 1.0, 0.0)
    r_i = lax.broadcasted_iota(jnp.int32, (tm, tm), 0)
    c_i = lax.broadcasted_iota(jnp.int32, (tm, tm), 1)
    earlier = jnp.where(r_i > c_i, 1.0, 0.0).astype(BF16)
    cum = jnp.dot(earlier, onehot.astype(BF16), preferred_element_type=F32) + run_ref[...]
    rank1 = jnp.sum(jnp.where(hit1, cum, 0.0), axis=1, keepdims=True).astype(jnp.int32)
    rank2 = jnp.sum(jnp.where(hit2, cum, 0.0), axis=1, keepdims=True).astype(jnp.int32)
    run_ref[...] = run_ref[...] + jnp.sum(onehot, axis=0, keepdims=True)
    cnt_ref[...] = run_ref[...]

    ri_ref[0] = jnp.where(lane == 0, i1.astype(jnp.int32), jnp.where(lane == 1, i2.astype(jnp.int32),
                          jnp.where(lane == 2, rank1, jnp.where(lane == 3, rank2, 0))))
    rw_ref[0] = jnp.where(lane == 0, w1, jnp.where(lane == 1, w2, 0.0))


def _mixer_out(att, a_out, gb, x, mod3, w_branch_b, w_out, g2, w_router, b_router):
    B, S, D = x.shape
    tm = TOKEN_TILE
    tok = lambda w: pl.BlockSpec((1, tm, w), lambda b, i: (b, i, 0))
    return pl.pallas_call(
        _mixer_out_kernel,
        grid=(B, S // tm),
        in_specs=[tok(ATT_WIDTH), tok(D), tok(D), tok(D),
                  pl.BlockSpec((1, N_MOD, D), lambda b, i: (b, 0, 0)),
                  _const_spec(w_branch_b.shape),
                  _const_spec(w_out.shape),
                  _const_spec((1, D)),
                  _const_spec(w_router.shape),
                  _const_spec((1, LANES))],
        out_specs=[tok(D), tok(D), tok(LANES), tok(LANES),
                   pl.BlockSpec((1, LANES), lambda b, i: (0, 0))],
        out_shape=[jax.ShapeDtypeStruct((B, S, D), F32),
                   jax.ShapeDtypeStruct((B, S, D), F32),
                   jax.ShapeDtypeStruct((B, S, LANES), jnp.int32),
                   jax.ShapeDtypeStruct((B, S, LANES), F32),
                   jax.ShapeDtypeStruct((1, LANES), F32)],
        scratch_shapes=[pltpu.VMEM((1, LANES), F32)],
        compiler_params=pltpu.CompilerParams(dimension_semantics=("arbitrary", "arbitrary"),
                                             vmem_limit_bytes=VMEM_LIMIT_BYTES),
        name="mixer_out",
    )(att, a_out, gb, x, mod3, w_branch_b, w_out, g2, w_router, b_router)


def _row_copies(src_ref, dst_ref, src_rows, dst_rows, r, sem):
    return pltpu.make_async_copy(src_ref.at[pl.ds(src_rows(r), 1), :], dst_ref.at[pl.ds(dst_rows(r), 1), :], sem)


def _dispatch_kernel(d1_ref, d2_ref, h_ref, xb_in_ref, xb_ref, sem):
    del xb_in_ref
    n = h_ref.shape[0]
    same = lambda r: r
    copies = [functools.partial(_row_copies, h_ref, xb_ref, same, lambda r, d=d: d[0, 0, r])
              for d in (d1_ref, d2_ref)]

    def start(r, carry):
        for cp in copies:
            cp(r, sem).start()
        return carry

    lax.fori_loop(0, n, start, 0, unroll=ROW_DMA_UNROLL)
    for _ in copies:
        pltpu.make_async_copy(h_ref, xb_ref.at[pl.ds(0, n), :], sem).wait()


def _dispatch(h2, dest1, dest2, n_rows):
    T, D = h2.shape
    n = ROW_DMA_TILE
    idx = pl.BlockSpec((1, 1, n), lambda i: (i, 0, 0), memory_space=pltpu.SMEM)
    return pl.pallas_call(
        _dispatch_kernel,
        grid=(T // n,),
        in_specs=[idx, idx,
                  pl.BlockSpec((n, D), lambda i: (i, 0)),
                  pl.BlockSpec(memory_space=pl.ANY)],
        out_specs=pl.BlockSpec(memory_space=pl.ANY),
        out_shape=jax.ShapeDtypeStruct((n_rows, D), F32),
        scratch_shapes=[pltpu.SemaphoreType.DMA(())],
        input_output_aliases={3: 0},
        compiler_params=pltpu.CompilerParams(dimension_semantics=("arbitrary",)),
        name="dispatch",
    )(dest1.reshape(T // n, 1, n), dest2.reshape(T // n, 1, n), h2, jnp.zeros((n_rows, D), F32))


def _expert_kernel(be_ref, nu_ref, x_ref, wg_ref, wu_ref, wd_ref, y_ref, wg_bf, wu_bf, wd_bf):
    i = pl.program_id(0)
    used = i < nu_ref[0]

    @pl.when(used & ((i == 0) | (be_ref[i] != be_ref[jnp.maximum(i - 1, 0)])))
    def _():
        wg_bf[...] = wg_ref[0].astype(BF16)
        wu_bf[...] = wu_ref[0].astype(BF16)
        wd_bf[...] = wd_ref[0].astype(BF16)

    @pl.when(used)
    def _():
        xb = x_ref[...].astype(BF16)
        gate = jnp.dot(xb, wg_bf[...], preferred_element_type=F32)
        up = jnp.dot(xb, wu_bf[...], preferred_element_type=F32)
        hid = (jax.nn.silu(gate) * up).astype(BF16)
        y_ref[...] = jnp.dot(hid, wd_bf[...], preferred_element_type=F32)

    @pl.when(i >= nu_ref[0])
    def _():
        y_ref[...] = jnp.zeros_like(y_ref)


def _experts(x_buf, block_expert, n_used, w_gate, w_up, w_down):
    P, D = x_buf.shape
    n_blocks = P // DISPATCH_BLOCK
    grid_spec = pltpu.PrefetchScalarGridSpec(
        num_scalar_prefetch=2,
        grid=(n_blocks,),
        in_specs=[pl.BlockSpec((DISPATCH_BLOCK, D), lambda i, be, nu: (i, 0)),
                  pl.BlockSpec((1, D, D_EXPERT), lambda i, be, nu: (be[i], 0, 0)),
                  pl.BlockSpec((1, D, D_EXPERT), lambda i, be, nu: (be[i], 0, 0)),
                  pl.BlockSpec((1, D_EXPERT, D), lambda i, be, nu: (be[i], 0, 0))],
        out_specs=pl.BlockSpec((DISPATCH_BLOCK, D), lambda i, be, nu: (i, 0)),
        scratch_shapes=[pltpu.VMEM((D, D_EXPERT), BF16), pltpu.VMEM((D, D_EXPERT), BF16),
                        pltpu.VMEM((D_EXPERT, D), BF16)],
    )
    return pl.pallas_call(
        _expert_kernel,
        grid_spec=grid_spec,
        out_shape=jax.ShapeDtypeStruct((P, D), F32),
        compiler_params=pltpu.CompilerParams(dimension_semantics=("arbitrary",),
                                             vmem_limit_bytes=VMEM_LIMIT_BYTES),
        name="experts",
    )(block_expert, n_used, x_buf, w_gate, w_up, w_down)


def _combine_kernel(d1_ref, d2_ref, d1n_ref, d2n_ref, x1_ref, rw_ref, mod_ref, fg_ref, yb_ref, o_ref, y_ref, sem):
    n = x1_ref.shape[1]
    step = pl.program_id(0) * pl.num_programs(1) + pl.program_id(1)
    last = pl.num_programs(0) * pl.num_programs(1) - 1
    slot = step % 2

    def gather(dests, buf):
        def start(r, carry):
            for k, d in enumerate(dests):
                pltpu.make_async_copy(yb_ref.at[pl.ds(d[0, 0, r], 1), :], y_ref.at[buf, k, pl.ds(r, 1), :],
                                      sem.at[buf]).start()
            return carry
        lax.fori_loop(0, n, start, 0, unroll=ROW_DMA_UNROLL)

    @pl.when(step == 0)
    def _():
        gather((d1_ref, d2_ref), 0)

    @pl.when(step < last)
    def _():
        gather((d1n_ref, d2n_ref), 1 - slot)

    for k in range(TOP_K_EXPERT):
        pltpu.make_async_copy(yb_ref.at[pl.ds(0, n), :], y_ref.at[slot, k], sem.at[slot]).wait()

    rw = rw_ref[0]
    moe = rw[:, 0:1] * y_ref[slot, 0] + rw[:, 1:2] * y_ref[slot, 1]
    g2 = mod_ref[0, 5:6, :]
    o_ref[0] = _rms(x1_ref[0] + g2 * moe) * fg_ref[...]


def _combine(x1, route_w, mod3, final_g, y_buf, dest1, dest2):
    B, S, D = x1.shape
    n = ROW_DMA_TILE
    nt = S // n
    idx = pl.BlockSpec((1, 1, n), lambda b, i: (b * nt + i, 0, 0), memory_space=pltpu.SMEM)
    idx_next = pl.BlockSpec((1, 1, n), lambda b, i: (jnp.minimum(b * nt + i + 1, B * nt - 1), 0, 0),
                            memory_space=pltpu.SMEM)
    tok = lambda w: pl.BlockSpec((1, n, w), lambda b, i: (b, i, 0))
    d1 = dest1.reshape(B * nt, 1, n)
    d2 = dest2.reshape(B * nt, 1, n)
    return pl.pallas_call(
        _combine_kernel,
        grid=(B, nt),
        in_specs=[idx, idx, idx_next, idx_next, tok(D), tok(LANES),
                  pl.BlockSpec((1, N_MOD, D), lambda b, i: (b, 0, 0)),
                  pl.BlockSpec((1, D), lambda b, i: (0, 0)),
                  pl.BlockSpec(memory_space=pl.ANY)],
        out_specs=tok(D),
        out_shape=jax.ShapeDtypeStruct((B, S, D), F32),
        scratch_shapes=[pltpu.VMEM((2, TOP_K_EXPERT, n, D), F32), pltpu.SemaphoreType.DMA((2,))],
        compiler_params=pltpu.CompilerParams(dimension_semantics=("arbitrary", "arbitrary")),
        name="combine",
    )(d1, d2, d1, d2, x1, route_w, mod3, final_g, y_buf)


def _rope_tables(S):
    half = ROPE_DIM // 2
    inv_freq = jnp.power(ROPE_THETA, -jnp.arange(half, dtype=jnp.float32) * 2.0 / ROPE_DIM)
    ang = jnp.arange(S).astype(jnp.float32)[:, None] * inv_freq[None, :]
    cos = jnp.cos(ang)
    sin = jnp.sin(ang)
    rest = HEAD_DIM - ROPE_DIM
    ones = jnp.ones((S, rest), F32)
    zeros = jnp.zeros((S, rest), F32)
    zh = jnp.zeros((S, half), F32)
    per_head = lambda parts: jnp.tile(jnp.concatenate(parts, axis=1), (1, HEADS_PER_LANE_TILE))
    return per_head([cos, cos, ones]), per_head([-sin, zh, zeros]), per_head([zh, sin, zeros])


def kernel(x, c, ada_w, ada_b, norm1_g, norm2_g, w_in, gmlp_ln_g, gmlp_ln_b, w_spatial, b_spatial, w_branch_a,
           w_branch_b, w_out, w_router_group, b_router_group, w_router_expert, b_router_expert, w_gate, w_up,
           w_down, final_norm_g):
    B, S, D = x.shape
    T = B * S
    assert D == D_MODEL and S % TOKEN_TILE == 0 and S % MOBA_BLOCK == 0 and S // MOBA_BLOCK <= 32
    assert ada_w.shape[0] == 1, "single layer"
    cos_t, sin_a, sin_b = _rope_tables(S)

    mod3 = _ada_mod(c, ada_w[0], ada_b).reshape(B, N_MOD, D)

    bias_sp = jnp.repeat(b_spatial[0].T, GMLP_CH, axis=1)
    q, k, v, a_out, gb, kmean = _mixer_in(
        x, mod3, norm1_g, w_in[0].astype(BF16), w_spatial[0], bias_sp, gmlp_ln_g, gmlp_ln_b,
        w_branch_a[0].astype(BF16), cos_t, sin_a, sin_b)
    att = _moba(q, k, v, kmean.reshape(B, S // MOBA_BLOCK, ATT_WIDTH))

    w_router = jnp.zeros((D, LANES), F32)
    w_router = w_router.at[:, :N_EXPERTS].set(w_router_expert[0].reshape(D, N_EXPERTS))
    w_router = w_router.at[:, ROUTER_GROUP_LANE0:ROUTER_GROUP_LANE0 + N_GROUPS].set(w_router_group[0])
    w_router_hi = w_router.astype(BF16)
    w_router = jnp.concatenate([w_router_hi, (w_router - w_router_hi.astype(F32)).astype(BF16)], axis=1)
    b_router = jnp.zeros((1, LANES), F32)
    b_router = b_router.at[0, :N_EXPERTS].set(b_router_expert[0].reshape(N_EXPERTS))
    b_router = b_router.at[0, ROUTER_GROUP_LANE0:ROUTER_GROUP_LANE0 + N_GROUPS].set(b_router_group[0])
    x1, h2, route_i, route_w, counts = _mixer_out(
        att, a_out, gb, x, mod3, w_branch_b[0].astype(BF16), w_out[0].astype(BF16), norm2_g, w_router, b_router)

    counts = counts[0, :N_EXPERTS].astype(jnp.int32)
    padded = (counts + DISPATCH_BLOCK - 1) // DISPATCH_BLOCK * DISPATCH_BLOCK
    pend = jnp.cumsum(padded)
    pstart = pend - padded
    n_blocks = -(-(T * TOP_K_EXPERT) // DISPATCH_BLOCK) + N_EXPERTS
    block_start = jnp.arange(n_blocks, dtype=jnp.int32) * DISPATCH_BLOCK
    block_expert = jnp.minimum(
        jnp.sum((pend[None, :] <= block_start[:, None]).astype(jnp.int32), axis=1), N_EXPERTS - 1)
    n_used = (pend[-1:] // DISPATCH_BLOCK).astype(jnp.int32)
    route_i = route_i.reshape(T, LANES)
    dest1 = pstart[route_i[:, 0]] + route_i[:, 2]
    dest2 = pstart[route_i[:, 1]] + route_i[:, 3]

    x_buf = _dispatch(h2.reshape(T, D), dest1, dest2, n_blocks * DISPATCH_BLOCK)
    y_buf = _experts(x_buf, block_expert, n_used, w_gate[0], w_up[0], w_down[0])
    return _combine(x1, route_w, mod3, final_norm_g.reshape(1, D), y_buf, dest1, dest2)
```

```python
import functools

import jax
import jax.numpy as jnp
from jax import lax
from jax.experimental import pallas as pl
from jax.experimental.pallas import tpu as pltpu

D_MODEL = 1024
GMLP_WIDTH = D_MODEL // 2
GMLP_GROUPS = 4
GMLP_CH = GMLP_WIDTH // GMLP_GROUPS
GMLP_CHUNK = 128
ATT_HEADS = 8
HEAD_DIM = 64
ATT_WIDTH = ATT_HEADS * HEAD_DIM
ROPE_DIM = HEAD_DIM // 4
ROPE_THETA = 500000.0
MOBA_BLOCK = 256
MOBA_TOPK = 3
N_GROUPS = 4
EXPERTS_PER_GROUP = 8
N_EXPERTS = N_GROUPS * EXPERTS_PER_GROUP
TOP_K_EXPERT = 2
D_EXPERT = D_MODEL // 2
DISPATCH_BLOCK = 256
N_MOD = 6
EPS = 1e-6

LANES = 128
HEADS_PER_LANE_TILE = LANES // HEAD_DIM
VMEM_LIMIT_BYTES = 56 * 1024 * 1024

TOKEN_TILE = 512
ROW_DMA_TILE = 256
ROW_DMA_UNROLL = 8

F32 = jnp.float32
BF16 = jnp.bfloat16
NEG_INF = float("-inf")
LOG2_E = 1.4426950408889634


def _rms(x):
    return x * lax.rsqrt(jnp.mean(x * x, axis=-1, keepdims=True) + EPS)


def _first_index_of(mask, idx, sentinel):
    return jnp.min(jnp.where(mask, idx, sentinel), axis=1, keepdims=True)


def _ada_kernel(c_ref, w_ref, b_ref, o_ref):
    o_ref[...] = jnp.dot(c_ref[...], w_ref[...], preferred_element_type=F32,
                         precision=lax.Precision.HIGHEST) + b_ref[...]


def _ada_mod(c, ada_w, ada_b):
    B = c.shape[0]
    return pl.pallas_call(
        _ada_kernel,
        grid=(N_MOD,),
        in_specs=[pl.BlockSpec((B, D_MODEL), lambda j: (0, 0)),
                  pl.BlockSpec((D_MODEL, D_MODEL), lambda j: (0, j)),
                  pl.BlockSpec((1, D_MODEL), lambda j: (0, j))],
        out_specs=pl.BlockSpec((B, D_MODEL), lambda j: (0, j)),
        out_shape=jax.ShapeDtypeStruct((B, N_MOD * D_MODEL), F32),
        name="ada_mod",
    )(c, ada_w, ada_b)


def _mixer_in_kernel(x_ref, mod_ref, g1_ref, w_in_ref, wsp_ref, bsp_ref, lng_ref, lnb_ref, wba_ref,
                     cos_ref, sa_ref, sb_ref,
                     q_ref, k_ref, v_ref, a_ref, gb_ref, km_ref):
    tm = x_ref.shape[1]
    sh1 = mod_ref[0, 0:1, :]
    sc1 = mod_ref[0, 1:2, :]
    h = (_rms(x_ref[0]) * g1_ref[...]) * (1.0 + sc1) + sh1
    hb = h.astype(BF16)

    z = jax.nn.gelu(jnp.dot(hb, w_in_ref[:, 0:2 * GMLP_WIDTH], preferred_element_type=F32))
    u = z[:, :GMLP_WIDTH]
    v = z[:, GMLP_WIDTH:]
    mu = jnp.mean(v, axis=-1, keepdims=True)
    var = jnp.mean(jnp.square(v - mu), axis=-1, keepdims=True)
    vn = ((v - mu) * lax.rsqrt(var + EPS) * lng_ref[...] + lnb_ref[...]).astype(BF16)
    t_idx = lax.broadcasted_iota(jnp.int32, (GMLP_CHUNK, GMLP_CHUNK), 0)
    s_idx = lax.broadcasted_iota(jnp.int32, (GMLP_CHUNK, GMLP_CHUNK), 1)
    w_causal = [jnp.where(t_idx >= s_idx, wsp_ref[g], 0.0).astype(BF16) for g in range(GMLP_GROUPS)]
    chunks = []
    for c in range(tm // GMLP_CHUNK):
        rows = slice(c * GMLP_CHUNK, (c + 1) * GMLP_CHUNK)
        cols = [jnp.dot(w_causal[g], vn[rows, g * GMLP_CH:(g + 1) * GMLP_CH], preferred_element_type=F32)
                for g in range(GMLP_GROUPS)]
        chunks.append(jnp.concatenate(cols, axis=1) + bsp_ref[...])
    sv = jnp.concatenate(chunks, axis=0)
    ya = jnp.dot((u * sv).astype(BF16), wba_ref[...], preferred_element_type=F32)

    off = 2 * GMLP_WIDTH + 3 * ATT_WIDTH
    ga = jnp.dot(hb, w_in_ref[:, off:off + D_MODEL], preferred_element_type=F32)
    a_ref[0] = (jax.nn.sigmoid(ga) * ya).astype(BF16)
    gbv = jnp.dot(hb, w_in_ref[:, off + D_MODEL:off + 2 * D_MODEL], preferred_element_type=F32)
    gb_ref[0] = jax.nn.sigmoid(gbv).astype(BF16)

    qkv = jnp.dot(hb, w_in_ref[:, 2 * GMLP_WIDTH:2 * GMLP_WIDTH + 3 * ATT_WIDTH], preferred_element_type=F32)
    cos_t = cos_ref[...]
    sin_a = sa_ref[...]
    sin_b = sb_ref[...]
    half = ROPE_DIM // 2

    def rope(t):
        outs = []
        for j in range(ATT_WIDTH // LANES):
            tj = t[:, j * LANES:(j + 1) * LANES]
            outs.append(tj * cos_t + pltpu.roll(tj, LANES - half, 1) * sin_a + pltpu.roll(tj, half, 1) * sin_b)
        return jnp.concatenate(outs, axis=1)

    q = rope(qkv[:, :ATT_WIDTH]) * (HEAD_DIM ** -0.5 * LOG2_E)
    k = rope(qkv[:, ATT_WIDTH:2 * ATT_WIDTH])
    q_ref[0] = q.astype(BF16)
    k_ref[0] = k.astype(BF16)
    v_ref[0] = qkv[:, 2 * ATT_WIDTH:].T.astype(BF16)
    for r in range(tm // MOBA_BLOCK):
        km_ref[0, 0, r:r + 1, :] = jnp.mean(k[r * MOBA_BLOCK:(r + 1) * MOBA_BLOCK], axis=0, keepdims=True)


def _const_spec(shape):
    nd = len(shape)
    return pl.BlockSpec(shape, lambda *_: (0,) * nd, pipeline_mode=pl.Buffered(1))


def _mixer_in(x, mod3, g1, w_in, w_spatial, bias_sp, ln_g, ln_b, w_branch_a, cos_t, sin_a, sin_b):
    B, S, D = x.shape
    tm = TOKEN_TILE
    nt = S // tm
    tok = lambda w: pl.BlockSpec((1, tm, w), lambda b, i: (b, i, 0))
    tab = pl.BlockSpec((tm, LANES), lambda b, i: (i, 0))
    return pl.pallas_call(
        _mixer_in_kernel,
        grid=(B, nt),
        in_specs=[tok(D),
                  pl.BlockSpec((1, N_MOD, D), lambda b, i: (b, 0, 0)),
                  _const_spec((1, D)),
                  _const_spec(w_in.shape),
                  _const_spec(w_spatial.shape),
                  _const_spec(bias_sp.shape),
                  _const_spec((1, GMLP_WIDTH)),
                  _const_spec((1, GMLP_WIDTH)),
                  _const_spec(w_branch_a.shape),
                  tab, tab, tab],
        out_specs=[tok(ATT_WIDTH), tok(ATT_WIDTH),
                   pl.BlockSpec((1, ATT_WIDTH, tm), lambda b, i: (b, 0, i)), tok(D), tok(D),
                   pl.BlockSpec((1, 1, tm // MOBA_BLOCK, ATT_WIDTH), lambda b, i: (b, i, 0, 0))],
        out_shape=[jax.ShapeDtypeStruct((B, S, ATT_WIDTH), BF16)] * 2
                  + [jax.ShapeDtypeStruct((B, ATT_WIDTH, S), BF16)]
                  + [jax.ShapeDtypeStruct((B, S, D), BF16)] * 2
                  + [jax.ShapeDtypeStruct((B, nt, tm // MOBA_BLOCK, ATT_WIDTH), F32)],
        compiler_params=pltpu.CompilerParams(dimension_semantics=("arbitrary", "arbitrary"),
                                             vmem_limit_bytes=VMEM_LIMIT_BYTES),
        name="mixer_in",
    )(x, mod3, g1, w_in, w_spatial, bias_sp, ln_g, ln_b, w_branch_a, cos_t, sin_a, sin_b)


def _moba_kernel(q_ref, k_ref, vt_ref, km_ref, o_ref, *scratch):
    tq = MOBA_BLOCK

    def tile(i, carry):
        rows = pl.ds(pl.multiple_of(i * tq, tq), tq)
        o_ref[0, rows, :] = _moba_tile(i, q_ref[0, rows, :], k_ref, vt_ref, km_ref, *scratch)
        return carry

    lax.fori_loop(0, q_ref.shape[1] // tq, tile, 0)


def _moba_tile(i, q2, k_ref, vt_ref, km_ref, m_ref, acc_ref, bits_ref,
               s0_ref, s1_ref, p0_ref, p1_ref, a0_ref, a1_ref):
    tq = q2.shape[0]
    nb = km_ref.shape[1]
    n_cols = HEADS_PER_LANE_TILE * tq
    lane = lax.broadcasted_iota(jnp.int32, (tq, LANES), 1)
    zero = jnp.zeros_like(q2)
    qs = jnp.concatenate([jnp.where(lane < HEAD_DIM, q2, zero), jnp.where(lane >= HEAD_DIM, q2, zero)], axis=0)
    nt_dims = (((1,), (1,)), ((), ()))

    km = km_ref[0]
    km_hi = km.astype(BF16)
    km_lo = (km - km_hi.astype(F32)).astype(BF16)
    gate = (lax.dot_general(km_hi, qs, nt_dims, preferred_element_type=F32)
            + lax.dot_general(km_lo, qs, nt_dims, preferred_element_type=F32))
    n_idx = lax.broadcasted_iota(jnp.int32, (nb, n_cols), 0)
    g = jnp.where(n_idx < i, gate, NEG_INF)
    bits = jnp.zeros((1, n_cols), jnp.int32)
    for r in range(MOBA_TOPK):
        mx = jnp.max(g, axis=0, keepdims=True)
        first = jnp.min(jnp.where(g == mx, n_idx, nb), axis=0, keepdims=True)
        bits = bits | jnp.where(r < i, jnp.left_shift(1, first), 0)
        g = jnp.where(n_idx == first, NEG_INF, g)
    bits_ref[...] = bits

    def block_start(t):
        return pl.multiple_of(jnp.where(t == 0, i, t - 1) * MOBA_BLOCK, MOBA_BLOCK)

    def scores(t, s_ref):
        t = jnp.minimum(t, i)
        s_ref[...] = lax.dot_general(k_ref[0, pl.ds(block_start(t), MOBA_BLOCK), :], qs, nt_dims,
                                     preferred_element_type=F32)

    def softmax_past(t, s_ref, p_ref, alpha_ref):
        j = jnp.minimum(t - 1, nb - 1)
        for c in range(n_cols // LANES):
            cols = slice(c * LANES, (c + 1) * LANES)
            s = s_ref[:, cols]
            picked = (lax.shift_right_logical(bits_ref[:, cols], j) & 1) == 1
            m_old = m_ref[:, cols]
            m_new = jnp.where(picked, jnp.maximum(m_old, jnp.max(s, axis=0, keepdims=True)), m_old)
            alpha_ref[:, cols] = jnp.exp2(m_old - m_new)
            m_sub = jnp.where(picked, m_new, jnp.inf)
            p_ref[:, cols] = jnp.exp2(s - m_sub).astype(BF16)
            m_ref[:, cols] = m_new

    def accumulate(t, p_ref, alpha_ref):
        vt = vt_ref[0, :, pl.ds(block_start(t), MOBA_BLOCK)]
        feat = lax.broadcasted_iota(jnp.int32, vt.shape, 0)
        one = jnp.ones_like(vt)
        v_heads = (jnp.where(feat < HEAD_DIM, vt, one), jnp.where(feat < HEAD_DIM, one, vt))
        for h in range(HEADS_PER_LANE_TILE):
            cols = slice(h * tq, (h + 1) * tq)
            acc_ref[:, cols] = alpha_ref[:, cols] * acc_ref[:, cols] + jnp.dot(
                v_heads[h], p_ref[:, cols], preferred_element_type=F32)

    s_refs, p_refs, alpha_refs = (s0_ref, s1_ref), (p0_ref, p1_ref), (a0_ref, a1_ref)

    scores(0, s_refs[0])
    key = lax.broadcasted_iota(jnp.int32, (MOBA_BLOCK, LANES), 0)
    qry = lax.broadcasted_iota(jnp.int32, (MOBA_BLOCK, LANES), 1)
    for c in range(n_cols // LANES):
        cols = slice(c * LANES, (c + 1) * LANES)
        s = jnp.where(key <= qry + (c * LANES) % tq, s_refs[0][:, cols], NEG_INF)
        m0 = jnp.max(s, axis=0, keepdims=True)
        p_refs[0][:, cols] = jnp.exp2(s - m0).astype(BF16)
        m_ref[:, cols] = m0
    alpha_refs[0][...] = jnp.ones_like(alpha_refs[0])
    acc_ref[...] = jnp.zeros_like(acc_ref)
    scores(1, s_refs[1])

    def tick_pair(d, carry):
        for half in range(2):
            t = 2 * d + 1 + half
            cur, prev = (1 - half), half
            accumulate(t - 1, p_refs[prev], alpha_refs[prev])
            softmax_past(t, s_refs[cur], p_refs[cur], alpha_refs[cur])
            scores(t + 1, s_refs[prev])
        return carry

    lax.fori_loop(0, (i + 2) // 2, tick_pair, 0)

    acc = acc_ref[...]
    out0 = acc[:HEAD_DIM, :tq] / acc[HEAD_DIM:, :tq]
    out1 = acc[HEAD_DIM:, tq:] / acc[:HEAD_DIM, tq:]
    return jnp.concatenate([out0, out1], axis=0).T.astype(BF16)


def _moba(q, k, vt, kmean):
    B, S, _ = q.shape
    tq = MOBA_BLOCK
    nb = S // MOBA_BLOCK
    n_pairs = ATT_HEADS // HEADS_PER_LANE_TILE
    n_cols = HEADS_PER_LANE_TILE * tq
    seq = pl.BlockSpec((1, S, LANES), lambda b, hp: (b, 0, hp))
    return pl.pallas_call(
        _moba_kernel,
        grid=(B, n_pairs),
        in_specs=[seq, seq,
                  pl.BlockSpec((1, LANES, S), lambda b, hp: (b, hp, 0)),
                  pl.BlockSpec((1, nb, LANES), lambda b, hp: (b, 0, hp))],
        out_specs=seq,
        out_shape=jax.ShapeDtypeStruct((B, S, ATT_WIDTH), BF16),
        scratch_shapes=[pltpu.VMEM((1, n_cols), F32),
                        pltpu.VMEM((LANES, n_cols), F32),
                        pltpu.VMEM((1, n_cols), jnp.int32),
                        pltpu.VMEM((MOBA_BLOCK, n_cols), F32),
                        pltpu.VMEM((MOBA_BLOCK, n_cols), F32),
                        pltpu.VMEM((MOBA_BLOCK, n_cols), BF16),
                        pltpu.VMEM((MOBA_BLOCK, n_cols), BF16),
                        pltpu.VMEM((1, n_cols), F32),
                        pltpu.VMEM((1, n_cols), F32)],
        compiler_params=pltpu.CompilerParams(dimension_semantics=("arbitrary", "arbitrary"),
                                             vmem_limit_bytes=VMEM_LIMIT_BYTES),
        name="moba",
    )(q, k, vt, kmean)


ROUTER_GROUP_LANE0 = N_EXPERTS


def _mixer_out_kernel(att_ref, a_ref, gb_ref, x_ref, mod_ref, wbb_ref, wout_ref, g2_ref, wr_ref, br_ref,
                      x1_ref, h2_ref, ri_ref, rw_ref, cnt_ref, run_ref):
    tm = x_ref.shape[1]

    @pl.when((pl.program_id(0) == 0) & (pl.program_id(1) == 0))
    def _():
        run_ref[...] = jnp.zeros_like(run_ref)

    yb = jnp.dot(att_ref[0], wbb_ref[...], preferred_element_type=F32)
    merged = a_ref[0].astype(F32) + gb_ref[0].astype(F32) * yb
    mo = jnp.dot(merged.astype(BF16), wout_ref[...], preferred_element_type=F32)
    g1 = mod_ref[0, 2:3, :]
    sh2 = mod_ref[0, 3:4, :]
    sc2 = mod_ref[0, 4:5, :]
    x1 = x_ref[0] + g1 * mo
    x1_ref[0] = x1
    h2 = (_rms(x1) * g2_ref[...]) * (1.0 + sc2) + sh2
    h2_ref[0] = h2

    h_hi = h2.astype(BF16)
    h_lo = (h2 - h_hi.astype(F32)).astype(BF16)
    parts = jnp.dot(jnp.concatenate([h_hi, h_lo], axis=0), wr_ref[...], preferred_element_type=F32)
    logits = (parts[:tm, :LANES] + parts[tm:, :LANES] + parts[:tm, LANES:]) + br_ref[...]
    lane = lax.broadcasted_iota(jnp.int32, (tm, LANES), 1)
    is_group = (lane >= ROUTER_GROUP_LANE0) & (lane < ROUTER_GROUP_LANE0 + N_GROUPS)
    gl = jnp.where(is_group, logits, NEG_INF)
    ge = jnp.exp(gl - jnp.max(gl, axis=1, keepdims=True))
    g_prob = jnp.where(is_group, ge / jnp.sum(ge, axis=1, keepdims=True), -1.0)
    g_p = jnp.max(g_prob, axis=1, keepdims=True)
    lane_f = lane.astype(F32)
    g_idx = _first_index_of(g_prob == g_p, lane_f, float(LANES)) - float(ROUTER_GROUP_LANE0)
    in_group = (lane_f >= g_idx * EXPERTS_PER_GROUP) & (lane_f < (g_idx + 1.0) * EXPERTS_PER_GROUP)
    el = jnp.where(in_group, logits, NEG_INF)
    ee = jnp.exp(el - jnp.max(el, axis=1, keepdims=True))
    e_prob = jnp.where(in_group, ee / jnp.sum(ee, axis=1, keepdims=True), -1.0)
    p1 = jnp.max(e_prob, axis=1, keepdims=True)
    i1 = _first_index_of(e_prob == p1, lane_f, float(LANES))
    e_prob2 = jnp.where(lane_f == i1, -1.0, e_prob)
    p2 = jnp.max(e_prob2, axis=1, keepdims=True)
    i2 = _first_index_of(e_prob2 == p2, lane_f, float(LANES))
    w1 = g_p * p1 / (p1 + p2)
    w2 = g_p * p2 / (p1 + p2)

    hit1 = lane_f == i1
    hit2 = lane_f == i2
    onehot = jnp.where(hit1 | hit2, 1.0, 0.0)
    r_i = lax.broadcasted_iota(jnp.int32, (tm, tm), 0)
    c_i = lax.broadcasted_iota(jnp.int32, (tm, tm), 1)
    earlier = jnp.where(r_i > c_i, 1.0, 0.0).astype(BF16)
    cum = jnp.dot(earlier, onehot.astype(BF16), preferred_element_type=F32) + run_ref[...]
    rank1 = jnp.sum(jnp.where(hit1, cum, 0.0), axis=1, keepdims=True).astype(jnp.int32)
    rank2 = jnp.sum(jnp.where(hit2, cum, 0.0), axis=1, keepdims=True).astype(jnp.int32)
    run_ref[...] = run_ref[...] + jnp.sum(onehot, axis=0, keepdims=True)
    cnt_ref[...] = run_ref[...]

    ri_ref[0] = jnp.where(lane == 0, i1.astype(jnp.int32), jnp.where(lane == 1, i2.astype(jnp.int32),
                          jnp.where(lane == 2, rank1, jnp.where(lane == 3, rank2, 0))))
    rw_ref[0] = jnp.where(lane == 0, w1, jnp.where(lane == 1, w2, 0.0))


def _mixer_out(att, a_out, gb, x, mod3, w_branch_b, w_out, g2, w_router, b_router):
    B, S, D = x.shape
    tm = TOKEN_TILE
    tok = lambda w: pl.BlockSpec((1, tm, w), lambda b, i: (b, i, 0))
    return pl.pallas_call(
        _mixer_out_kernel,
        grid=(B, S // tm),
        in_specs=[tok(ATT_WIDTH), tok(D), tok(D), tok(D),
                  pl.BlockSpec((1, N_MOD, D), lambda b, i: (b, 0, 0)),
                  _const_spec(w_branch_b.shape),
                  _const_spec(w_out.shape),
                  _const_spec((1, D)),
                  _const_spec(w_router.shape),
                  _const_spec((1, LANES))],
        out_specs=[tok(D), tok(D), tok(LANES), tok(LANES),
                   pl.BlockSpec((1, LANES), lambda b, i: (0, 0))],
        out_shape=[jax.ShapeDtypeStruct((B, S, D), F32),
                   jax.ShapeDtypeStruct((B, S, D), F32),
                   jax.ShapeDtypeStruct((B, S, LANES), jnp.int32),
                   jax.ShapeDtypeStruct((B, S, LANES), F32),
                   jax.ShapeDtypeStruct((1, LANES), F32)],
        scratch_shapes=[pltpu.VMEM((1, LANES), F32)],
        compiler_params=pltpu.CompilerParams(dimension_semantics=("arbitrary", "arbitrary"),
                                             vmem_limit_bytes=VMEM_LIMIT_BYTES),
        name="mixer_out",
    )(att, a_out, gb, x, mod3, w_branch_b, w_out, g2, w_router, b_router)


def _row_copies(src_ref, dst_ref, src_rows, dst_rows, r, sem):
    return pltpu.make_async_copy(src_ref.at[pl.ds(src_rows(r), 1), :], dst_ref.at[pl.ds(dst_rows(r), 1), :], sem)


def _dispatch_kernel(tail_ref, d1_ref, d2_ref, h_ref, xb_ref, zero_ref, sem, zero_sem):
    n = h_ref.shape[0]

    @pl.when(pl.program_id(0) == 0)
    def _():
        zero_ref[...] = jnp.zeros_like(zero_ref)
        zero_block = lambda e: pltpu.make_async_copy(
            zero_ref, xb_ref.at[pl.ds(pl.multiple_of(tail_ref[e], DISPATCH_BLOCK), DISPATCH_BLOCK), :], zero_sem)
        for phase in ("start", "wait"):
            for e in range(tail_ref.shape[0]):
                @pl.when(tail_ref[e] >= 0)
                def _(e=e, phase=phase):
                    getattr(zero_block(e), phase)()

    same = lambda r: r
    copies = [functools.partial(_row_copies, h_ref, xb_ref, same, lambda r, d=d: d[0, 0, r])
              for d in (d1_ref, d2_ref)]

    def start(r, carry):
        for cp in copies:
            cp(r, sem).start()
        return carry

    lax.fori_loop(0, n, start, 0, unroll=ROW_DMA_UNROLL)
    for _ in copies:
        pltpu.make_async_copy(h_ref, xb_ref.at[pl.ds(0, n), :], sem).wait()


def _dispatch(h2, dest1, dest2, tail_block_start, n_rows):
    T, D = h2.shape
    n = ROW_DMA_TILE
    idx = pl.BlockSpec((1, 1, n), lambda i, tail: (i, 0, 0), memory_space=pltpu.SMEM)
    grid_spec = pltpu.PrefetchScalarGridSpec(
        num_scalar_prefetch=1,
        grid=(T // n,),
        in_specs=[idx, idx, pl.BlockSpec((n, D), lambda i, tail: (i, 0))],
        out_specs=pl.BlockSpec(memory_space=pl.ANY),
        scratch_shapes=[pltpu.VMEM((DISPATCH_BLOCK, D), F32), pltpu.SemaphoreType.DMA(()),
                        pltpu.SemaphoreType.DMA(())],
    )
    return pl.pallas_call(
        _dispatch_kernel,
        grid_spec=grid_spec,
        out_shape=jax.ShapeDtypeStruct((n_rows, D), F32),
        compiler_params=pltpu.CompilerParams(dimension_semantics=("arbitrary",)),
        name="dispatch",
    )(tail_block_start, dest1.reshape(T // n, 1, n), dest2.reshape(T // n, 1, n), h2)


def _expert_kernel(be_ref, nu_ref, x_ref, wg_ref, wu_ref, wd_ref, y_ref, wg_bf, wu_bf, wd_bf):
    i = pl.program_id(0)
    used = i < nu_ref[0]

    @pl.when(used & ((i == 0) | (be_ref[i] != be_ref[jnp.maximum(i - 1, 0)])))
    def _():
        wg_bf[...] = wg_ref[0].astype(BF16)
        wu_bf[...] = wu_ref[0].astype(BF16)
        wd_bf[...] = wd_ref[0].astype(BF16)

    @pl.when(used)
    def _():
        xb = x_ref[...].astype(BF16)
        gate = jnp.dot(xb, wg_bf[...], preferred_element_type=F32)
        up = jnp.dot(xb, wu_bf[...], preferred_element_type=F32)
        hid = (jax.nn.silu(gate) * up).astype(BF16)
        y_ref[...] = jnp.dot(hid, wd_bf[...], preferred_element_type=F32)

    @pl.when(i >= nu_ref[0])
    def _():
        y_ref[...] = jnp.zeros_like(y_ref)


def _experts(x_buf, block_expert, n_used, w_gate, w_up, w_down):
    P, D = x_buf.shape
    n_blocks = P // DISPATCH_BLOCK
    grid_spec = pltpu.PrefetchScalarGridSpec(
        num_scalar_prefetch=2,
        grid=(n_blocks,),
        in_specs=[pl.BlockSpec((DISPATCH_BLOCK, D), lambda i, be, nu: (jnp.minimum(i, nu[0] - 1), 0)),
                  pl.BlockSpec((1, D, D_EXPERT), lambda i, be, nu: (be[i], 0, 0)),
                  pl.BlockSpec((1, D, D_EXPERT), lambda i, be, nu: (be[i], 0, 0)),
                  pl.BlockSpec((1, D_EXPERT, D), lambda i, be, nu: (be[i], 0, 0))],
        out_specs=pl.BlockSpec((DISPATCH_BLOCK, D), lambda i, be, nu: (i, 0)),
        scratch_shapes=[pltpu.VMEM((D, D_EXPERT), BF16), pltpu.VMEM((D, D_EXPERT), BF16),
                        pltpu.VMEM((D_EXPERT, D), BF16)],
    )
    return pl.pallas_call(
        _expert_kernel,
        grid_spec=grid_spec,
        out_shape=jax.ShapeDtypeStruct((P, D), F32),
        compiler_params=pltpu.CompilerParams(dimension_semantics=("arbitrary",),
                                             vmem_limit_bytes=VMEM_LIMIT_BYTES),
        name="experts",
    )(block_expert, n_used, x_buf, w_gate, w_up, w_down)


def _combine_kernel(d1_ref, d2_ref, d1n_ref, d2n_ref, x1_ref, rw_ref, mod_ref, fg_ref, yb_ref, o_ref, y_ref, sem):
    n = x1_ref.shape[1]
    step = pl.program_id(0) * pl.num_programs(1) + pl.program_id(1)
    last = pl.num_programs(0) * pl.num_programs(1) - 1
    slot = step % 2

    def gather(dests, buf):
        def start(r, carry):
            for k, d in enumerate(dests):
                pltpu.make_async_copy(yb_ref.at[pl.ds(d[0, 0, r], 1), :], y_ref.at[buf, k, pl.ds(r, 1), :],
                                      sem.at[buf]).start()
            return carry
        lax.fori_loop(0, n, start, 0, unroll=ROW_DMA_UNROLL)

    @pl.when(step == 0)
    def _():
        gather((d1_ref, d2_ref), 0)

    @pl.when(step < last)
    def _():
        gather((d1n_ref, d2n_ref), 1 - slot)

    for k in range(TOP_K_EXPERT):
        pltpu.make_async_copy(yb_ref.at[pl.ds(0, n), :], y_ref.at[slot, k], sem.at[slot]).wait()

    rw = rw_ref[0]
    moe = rw[:, 0:1] * y_ref[slot, 0] + rw[:, 1:2] * y_ref[slot, 1]
    g2 = mod_ref[0, 5:6, :]
    o_ref[0] = _rms(x1_ref[0] + g2 * moe) * fg_ref[...]


def _combine(x1, route_w, mod3, final_g, y_buf, dest1, dest2):
    B, S, D = x1.shape
    n = ROW_DMA_TILE
    nt = S // n
    idx = pl.BlockSpec((1, 1, n), lambda b, i: (b * nt + i, 0, 0), memory_space=pltpu.SMEM)
    idx_next = pl.BlockSpec((1, 1, n), lambda b, i: (jnp.minimum(b * nt + i + 1, B * nt - 1), 0, 0),
                            memory_space=pltpu.SMEM)
    tok = lambda w: pl.BlockSpec((1, n, w), lambda b, i: (b, i, 0))
    d1 = dest1.reshape(B * nt, 1, n)
    d2 = dest2.reshape(B * nt, 1, n)
    return pl.pallas_call(
        _combine_kernel,
        grid=(B, nt),
        in_specs=[idx, idx, idx_next, idx_next, tok(D), tok(LANES),
                  pl.BlockSpec((1, N_MOD, D), lambda b, i: (b, 0, 0)),
                  pl.BlockSpec((1, D), lambda b, i: (0, 0)),
                  pl.BlockSpec(memory_space=pl.ANY)],
        out_specs=tok(D),
        out_shape=jax.ShapeDtypeStruct((B, S, D), F32),
        scratch_shapes=[pltpu.VMEM((2, TOP_K_EXPERT, n, D), F32), pltpu.SemaphoreType.DMA((2,))],
        compiler_params=pltpu.CompilerParams(dimension_semantics=("arbitrary", "arbitrary")),
        name="combine",
    )(d1, d2, d1, d2, x1, route_w, mod3, final_g, y_buf)


def _rope_tables(S):
    half = ROPE_DIM // 2
    inv_freq = jnp.power(ROPE_THETA, -jnp.arange(half, dtype=jnp.float32) * 2.0 / ROPE_DIM)
    ang = jnp.arange(S).astype(jnp.float32)[:, None] * inv_freq[None, :]
    cos = jnp.cos(ang)
    sin = jnp.sin(ang)
    rest = HEAD_DIM - ROPE_DIM
    ones = jnp.ones((S, rest), F32)
    zeros = jnp.zeros((S, rest), F32)
    zh = jnp.zeros((S, half), F32)
    per_head = lambda parts: jnp.tile(jnp.concatenate(parts, axis=1), (1, HEADS_PER_LANE_TILE))
    return per_head([cos, cos, ones]), per_head([-sin, zh, zeros]), per_head([zh, sin, zeros])


def kernel(x, c, ada_w, ada_b, norm1_g, norm2_g, w_in, gmlp_ln_g, gmlp_ln_b, w_spatial, b_spatial, w_branch_a,
           w_branch_b, w_out, w_router_group, b_router_group, w_router_expert, b_router_expert, w_gate, w_up,
           w_down, final_norm_g):
    B, S, D = x.shape
    T = B * S
    assert D == D_MODEL and S % TOKEN_TILE == 0 and S % MOBA_BLOCK == 0 and S // MOBA_BLOCK <= 32
    assert ada_w.shape[0] == 1, "single layer"
    cos_t, sin_a, sin_b = _rope_tables(S)

    mod3 = _ada_mod(c, ada_w[0], ada_b).reshape(B, N_MOD, D)

    bias_sp = jnp.repeat(b_spatial[0].T, GMLP_CH, axis=1)
    q, k, v, a_out, gb, kmean = _mixer_in(
        x, mod3, norm1_g, w_in[0].astype(BF16), w_spatial[0], bias_sp, gmlp_ln_g, gmlp_ln_b,
        w_branch_a[0].astype(BF16), cos_t, sin_a, sin_b)
    att = _moba(q, k, v, kmean.reshape(B, S // MOBA_BLOCK, ATT_WIDTH))

    w_router = jnp.zeros((D, LANES), F32)
    w_router = w_router.at[:, :N_EXPERTS].set(w_router_expert[0].reshape(D, N_EXPERTS))
    w_router = w_router.at[:, ROUTER_GROUP_LANE0:ROUTER_GROUP_LANE0 + N_GROUPS].set(w_router_group[0])
    w_router_hi = w_router.astype(BF16)
    w_router = jnp.concatenate([w_router_hi, (w_router - w_router_hi.astype(F32)).astype(BF16)], axis=1)
    b_router = jnp.zeros((1, LANES), F32)
    b_router = b_router.at[0, :N_EXPERTS].set(b_router_expert[0].reshape(N_EXPERTS))
    b_router = b_router.at[0, ROUTER_GROUP_LANE0:ROUTER_GROUP_LANE0 + N_GROUPS].set(b_router_group[0])
    x1, h2, route_i, route_w, counts = _mixer_out(
        att, a_out, gb, x, mod3, w_branch_b[0].astype(BF16), w_out[0].astype(BF16), norm2_g, w_router, b_router)

    counts = counts[0, :N_EXPERTS].astype(jnp.int32)
    padded = (counts + DISPATCH_BLOCK - 1) // DISPATCH_BLOCK * DISPATCH_BLOCK
    pend = jnp.cumsum(padded)
    pstart = pend - padded
    n_blocks = -(-(T * TOP_K_EXPERT) // DISPATCH_BLOCK) + N_EXPERTS
    block_start = jnp.arange(n_blocks, dtype=jnp.int32) * DISPATCH_BLOCK
    block_expert = jnp.minimum(
        jnp.sum((pend[None, :] <= block_start[:, None]).astype(jnp.int32), axis=1), N_EXPERTS - 1)
    n_used = (pend[-1:] // DISPATCH_BLOCK).astype(jnp.int32)
    route_i = route_i.reshape(T, LANES)
    dest1 = pstart[route_i[:, 0]] + route_i[:, 2]
    dest2 = pstart[route_i[:, 1]] + route_i[:, 3]

    spare = (n_used[0] + jnp.arange(N_EXPERTS, dtype=jnp.int32)) * DISPATCH_BLOCK
    tail_block_start = jnp.concatenate([
        jnp.where(counts > 0, pend - DISPATCH_BLOCK, -1),
        jnp.where(spare < n_blocks * DISPATCH_BLOCK, spare, -1)]).astype(jnp.int32)
    x_buf = _dispatch(h2.reshape(T, D), dest1, dest2, tail_block_start, n_blocks * DISPATCH_BLOCK)
    y_buf = _experts(x_buf, block_expert, n_used, w_gate[0], w_up[0], w_down[0])
    return _combine(x1, route_w, mod3, final_norm_g.reshape(1, D), y_buf, dest1, dest2)
```

```python
import functools

import jax
import jax.numpy as jnp
from jax import lax
from jax.experimental import pallas as pl
from jax.experimental.pallas import tpu as pltpu

D_MODEL = 1024
GMLP_WIDTH = D_MODEL // 2
GMLP_GROUPS = 4
GMLP_CH = GMLP_WIDTH // GMLP_GROUPS
GMLP_CHUNK = 128
ATT_HEADS = 8
HEAD_DIM = 64
ATT_WIDTH = ATT_HEADS * HEAD_DIM
ROPE_DIM = HEAD_DIM // 4
ROPE_THETA = 500000.0
MOBA_BLOCK = 256
MOBA_TOPK = 3
N_GROUPS = 4
EXPERTS_PER_GROUP = 8
N_EXPERTS = N_GROUPS * EXPERTS_PER_GROUP
TOP_K_EXPERT = 2
D_EXPERT = D_MODEL // 2
DISPATCH_BLOCK = 256
N_MOD = 6
EPS = 1e-6

LANES = 128
HEADS_PER_LANE_TILE = LANES // HEAD_DIM
VMEM_LIMIT_BYTES = 56 * 1024 * 1024

TOKEN_TILE = 512
ROW_DMA_TILE = 256
ROW_DMA_UNROLL = 8

F32 = jnp.float32
BF16 = jnp.bfloat16
NEG_INF = float("-inf")
LOG2_E = 1.4426950408889634


def _rms(x):
    return x * lax.rsqrt(jnp.mean(x * x, axis=-1, keepdims=True) + EPS)


def _first_index_of(mask, idx, sentinel):
    return jnp.min(jnp.where(mask, idx, sentinel), axis=1, keepdims=True)


def _ada_kernel(c_ref, w_ref, b_ref, o_ref):
    o_ref[...] = jnp.dot(c_ref[...], w_ref[...], preferred_element_type=F32,
                         precision=lax.Precision.HIGHEST) + b_ref[...]


def _ada_mod(c, ada_w, ada_b):
    B = c.shape[0]
    return pl.pallas_call(
        _ada_kernel,
        grid=(N_MOD,),
        in_specs=[pl.BlockSpec((B, D_MODEL), lambda j: (0, 0)),
                  pl.BlockSpec((D_MODEL, D_MODEL), lambda j: (0, j)),
                  pl.BlockSpec((1, D_MODEL), lambda j: (0, j))],
        out_specs=pl.BlockSpec((B, D_MODEL), lambda j: (0, j)),
        out_shape=jax.ShapeDtypeStruct((B, N_MOD * D_MODEL), F32),
        name="ada_mod",
    )(c, ada_w, ada_b)


def _mixer_in_kernel(x_ref, mod_ref, g1_ref, w_in_ref, wsp_ref, bsp_ref, lng_ref, lnb_ref, wba_ref,
                     cos_ref, sa_ref, sb_ref,
                     q_ref, k_ref, v_ref, a_ref, gb_ref, km_ref):
    tm = x_ref.shape[1]
    sh1 = mod_ref[0, 0:1, :]
    sc1 = mod_ref[0, 1:2, :]
    h = (_rms(x_ref[0]) * g1_ref[...]) * (1.0 + sc1) + sh1
    hb = h.astype(BF16)

    z = jax.nn.gelu(jnp.dot(hb, w_in_ref[:, 0:2 * GMLP_WIDTH], preferred_element_type=F32))
    u = z[:, :GMLP_WIDTH]
    v = z[:, GMLP_WIDTH:]
    mu = jnp.mean(v, axis=-1, keepdims=True)
    var = jnp.mean(jnp.square(v - mu), axis=-1, keepdims=True)
    vn = ((v - mu) * lax.rsqrt(var + EPS) * lng_ref[...] + lnb_ref[...]).astype(BF16)
    t_idx = lax.broadcasted_iota(jnp.int32, (GMLP_CHUNK, GMLP_CHUNK), 0)
    s_idx = lax.broadcasted_iota(jnp.int32, (GMLP_CHUNK, GMLP_CHUNK), 1)
    w_causal = [jnp.where(t_idx >= s_idx, wsp_ref[g], 0.0).astype(BF16) for g in range(GMLP_GROUPS)]
    chunks = []
    for c in range(tm // GMLP_CHUNK):
        rows = slice(c * GMLP_CHUNK, (c + 1) * GMLP_CHUNK)
        cols = [jnp.dot(w_causal[g], vn[rows, g * GMLP_CH:(g + 1) * GMLP_CH], preferred_element_type=F32)
                for g in range(GMLP_GROUPS)]
        chunks.append(jnp.concatenate(cols, axis=1) + bsp_ref[...])
    sv = jnp.concatenate(chunks, axis=0)
    ya = jnp.dot((u * sv).astype(BF16), wba_ref[...], preferred_element_type=F32)

    off = 2 * GMLP_WIDTH + 3 * ATT_WIDTH
    ga = jnp.dot(hb, w_in_ref[:, off:off + D_MODEL], preferred_element_type=F32)
    a_ref[0] = (jax.nn.sigmoid(ga) * ya).astype(BF16)
    gbv = jnp.dot(hb, w_in_ref[:, off + D_MODEL:off + 2 * D_MODEL], preferred_element_type=F32)
    gb_ref[0] = jax.nn.sigmoid(gbv).astype(BF16)

    qkv = jnp.dot(hb, w_in_ref[:, 2 * GMLP_WIDTH:2 * GMLP_WIDTH + 3 * ATT_WIDTH], preferred_element_type=F32)
    cos_t = cos_ref[...]
    sin_a = sa_ref[...]
    sin_b = sb_ref[...]
    half = ROPE_DIM // 2

    def rope(t):
        outs = []
        for j in range(ATT_WIDTH // LANES):
            tj = t[:, j * LANES:(j + 1) * LANES]
            outs.append(tj * cos_t + pltpu.roll(tj, LANES - half, 1) * sin_a + pltpu.roll(tj, half, 1) * sin_b)
        return jnp.concatenate(outs, axis=1)

    q = rope(qkv[:, :ATT_WIDTH]) * (HEAD_DIM ** -0.5 * LOG2_E)
    k = rope(qkv[:, ATT_WIDTH:2 * ATT_WIDTH])
    q_ref[0] = q.astype(BF16)
    k_ref[0] = k.astype(BF16)
    v_ref[0] = qkv[:, 2 * ATT_WIDTH:].T.astype(BF16)
    for r in range(tm // MOBA_BLOCK):
        km_ref[0, 0, r:r + 1, :] = jnp.mean(k[r * MOBA_BLOCK:(r + 1) * MOBA_BLOCK], axis=0, keepdims=True)


def _const_spec(shape):
    nd = len(shape)
    return pl.BlockSpec(shape, lambda *_: (0,) * nd, pipeline_mode=pl.Buffered(1))


def _mixer_in(x, mod3, g1, w_in, w_spatial, bias_sp, ln_g, ln_b, w_branch_a, cos_t, sin_a, sin_b):
    B, S, D = x.shape
    tm = TOKEN_TILE
    nt = S // tm
    tok = lambda w: pl.BlockSpec((1, tm, w), lambda b, i: (b, i, 0))
    tab = pl.BlockSpec((tm, LANES), lambda b, i: (i, 0))
    return pl.pallas_call(
        _mixer_in_kernel,
        grid=(B, nt),
        in_specs=[tok(D),
                  pl.BlockSpec((1, N_MOD, D), lambda b, i: (b, 0, 0)),
                  _const_spec((1, D)),
                  _const_spec(w_in.shape),
                  _const_spec(w_spatial.shape),
                  _const_spec(bias_sp.shape),
                  _const_spec((1, GMLP_WIDTH)),
                  _const_spec((1, GMLP_WIDTH)),
                  _const_spec(w_branch_a.shape),
                  tab, tab, tab],
        out_specs=[tok(ATT_WIDTH), tok(ATT_WIDTH),
                   pl.BlockSpec((1, ATT_WIDTH, tm), lambda b, i: (b, 0, i)), tok(D), tok(D),
                   pl.BlockSpec((1, 1, tm // MOBA_BLOCK, ATT_WIDTH), lambda b, i: (b, i, 0, 0))],
        out_shape=[jax.ShapeDtypeStruct((B, S, ATT_WIDTH), BF16)] * 2
                  + [jax.ShapeDtypeStruct((B, ATT_WIDTH, S), BF16)]
                  + [jax.ShapeDtypeStruct((B, S, D), BF16)] * 2
                  + [jax.ShapeDtypeStruct((B, nt, tm // MOBA_BLOCK, ATT_WIDTH), F32)],
        compiler_params=pltpu.CompilerParams(dimension_semantics=("arbitrary", "arbitrary"),
                                             vmem_limit_bytes=VMEM_LIMIT_BYTES),
        name="mixer_in",
    )(x, mod3, g1, w_in, w_spatial, bias_sp, ln_g, ln_b, w_branch_a, cos_t, sin_a, sin_b)


def _moba_kernel(q_ref, k_ref, vt_ref, km_ref, o_ref, *scratch):
    tq = MOBA_BLOCK

    def tile(i, carry):
        rows = pl.ds(pl.multiple_of(i * tq, tq), tq)
        o_ref[0, rows, :] = _moba_tile(i, q_ref[0, rows, :], k_ref, vt_ref, km_ref, *scratch)
        return carry

    lax.fori_loop(0, q_ref.shape[1] // tq, tile, 0)


def _moba_tile(i, q2, k_ref, vt_ref, km_ref, m_ref, acc_ref, bits_ref,
               s0_ref, s1_ref, p0_ref, p1_ref, a0_ref, a1_ref):
    tq = q2.shape[0]
    nb = km_ref.shape[1]
    n_cols = HEADS_PER_LANE_TILE * tq
    q_t = q2.astype(F32).T
    feat = lax.broadcasted_iota(jnp.int32, (LANES, tq), 0)
    qs_t = jnp.concatenate([jnp.where(feat < HEAD_DIM, q_t, 0.0), jnp.where(feat >= HEAD_DIM, q_t, 0.0)],
                           axis=1).astype(BF16)

    km = km_ref[0]
    km_hi = km.astype(BF16)
    km_lo = (km - km_hi.astype(F32)).astype(BF16)
    gate = (jnp.dot(km_hi, qs_t, preferred_element_type=F32)
            + jnp.dot(km_lo, qs_t, preferred_element_type=F32))
    n_idx = lax.broadcasted_iota(jnp.int32, (nb, n_cols), 0)
    g = jnp.where(n_idx < i, gate, NEG_INF)
    bits = jnp.zeros((1, n_cols), jnp.int32)
    for r in range(MOBA_TOPK):
        mx = jnp.max(g, axis=0, keepdims=True)
        first = jnp.min(jnp.where(g == mx, n_idx, nb), axis=0, keepdims=True)
        bits = bits | jnp.where(r < i, jnp.left_shift(1, first), 0)
        g = jnp.where(n_idx == first, NEG_INF, g)
    bits_ref[...] = bits

    def block_start(t):
        return pl.multiple_of(jnp.where(t == 0, i, t - 1) * MOBA_BLOCK, MOBA_BLOCK)

    def scores(t, s_ref):
        t = jnp.minimum(t, i)
        s_ref[...] = jnp.dot(k_ref[0, pl.ds(block_start(t), MOBA_BLOCK), :], qs_t, preferred_element_type=F32)

    def softmax_past(t, s_ref, p_ref, alpha_ref):
        j = jnp.minimum(t - 1, nb - 1)
        for c in range(n_cols // LANES):
            cols = slice(c * LANES, (c + 1) * LANES)
            s = s_ref[:, cols]
            picked = (lax.shift_right_logical(bits_ref[:, cols], j) & 1) == 1
            m_old = m_ref[:, cols]
            m_new = jnp.where(picked, jnp.maximum(m_old, jnp.max(s, axis=0, keepdims=True)), m_old)
            alpha_ref[:, cols] = jnp.exp2(m_old - m_new)
            m_sub = jnp.where(picked, m_new, jnp.inf)
            p_ref[:, cols] = jnp.exp2(s - m_sub).astype(BF16)
            m_ref[:, cols] = m_new

    def accumulate(t, p_ref, alpha_ref):
        vt = vt_ref[0, :, pl.ds(block_start(t), MOBA_BLOCK)]
        feat = lax.broadcasted_iota(jnp.int32, vt.shape, 0)
        one = jnp.ones_like(vt)
        v_heads = (jnp.where(feat < HEAD_DIM, vt, one), jnp.where(feat < HEAD_DIM, one, vt))
        for h in range(HEADS_PER_LANE_TILE):
            cols = slice(h * tq, (h + 1) * tq)
            acc_ref[:, cols] = alpha_ref[:, cols] * acc_ref[:, cols] + jnp.dot(
                v_heads[h], p_ref[:, cols], preferred_element_type=F32)

    s_refs, p_refs, alpha_refs = (s0_ref, s1_ref), (p0_ref, p1_ref), (a0_ref, a1_ref)

    scores(0, s_refs[0])
    key = lax.broadcasted_iota(jnp.int32, (MOBA_BLOCK, LANES), 0)
    qry = lax.broadcasted_iota(jnp.int32, (MOBA_BLOCK, LANES), 1)
    for c in range(n_cols // LANES):
        cols = slice(c * LANES, (c + 1) * LANES)
        s = jnp.where(key <= qry + (c * LANES) % tq, s_refs[0][:, cols], NEG_INF)
        m0 = jnp.max(s, axis=0, keepdims=True)
        p_refs[0][:, cols] = jnp.exp2(s - m0).astype(BF16)
        m_ref[:, cols] = m0
    alpha_refs[0][...] = jnp.ones_like(alpha_refs[0])
    acc_ref[...] = jnp.zeros_like(acc_ref)
    scores(1, s_refs[1])

    def tick_pair(d, carry):
        for half in range(2):
            t = 2 * d + 1 + half
            cur, prev = (1 - half), half
            accumulate(t - 1, p_refs[prev], alpha_refs[prev])
            softmax_past(t, s_refs[cur], p_refs[cur], alpha_refs[cur])
            scores(t + 1, s_refs[prev])
        return carry

    lax.fori_loop(0, i // 2, tick_pair, 0)

    @pl.when(i % 2 == 1)
    def _():
        accumulate(i - 1, p_refs[0], alpha_refs[0])
        softmax_past(i, s_refs[1], p_refs[1], alpha_refs[1])
        accumulate(i, p_refs[1], alpha_refs[1])

    @pl.when(i % 2 == 0)
    def _():
        accumulate(i, p_refs[0], alpha_refs[0])

    acc = acc_ref[...]
    out0 = acc[:HEAD_DIM, :tq] / acc[HEAD_DIM:, :tq]
    out1 = acc[HEAD_DIM:, tq:] / acc[:HEAD_DIM, tq:]
    return jnp.concatenate([out0, out1], axis=0).T.astype(BF16)


def _moba(q, k, vt, kmean):
    B, S, _ = q.shape
    tq = MOBA_BLOCK
    nb = S // MOBA_BLOCK
    n_pairs = ATT_HEADS // HEADS_PER_LANE_TILE
    n_cols = HEADS_PER_LANE_TILE * tq
    seq = pl.BlockSpec((1, S, LANES), lambda b, hp: (b, 0, hp))
    return pl.pallas_call(
        _moba_kernel,
        grid=(B, n_pairs),
        in_specs=[seq, seq,
                  pl.BlockSpec((1, LANES, S), lambda b, hp: (b, hp, 0)),
                  pl.BlockSpec((1, nb, LANES), lambda b, hp: (b, 0, hp))],
        out_specs=seq,
        out_shape=jax.ShapeDtypeStruct((B, S, ATT_WIDTH), BF16),
        scratch_shapes=[pltpu.VMEM((1, n_cols), F32),
                        pltpu.VMEM((LANES, n_cols), F32),
                        pltpu.VMEM((1, n_cols), jnp.int32),
                        pltpu.VMEM((MOBA_BLOCK, n_cols), F32),
                        pltpu.VMEM((MOBA_BLOCK, n_cols), F32),
                        pltpu.VMEM((MOBA_BLOCK, n_cols), BF16),
                        pltpu.VMEM((MOBA_BLOCK, n_cols), BF16),
                        pltpu.VMEM((1, n_cols), F32),
                        pltpu.VMEM((1, n_cols), F32)],
        compiler_params=pltpu.CompilerParams(dimension_semantics=("arbitrary", "arbitrary"),
                                             vmem_limit_bytes=VMEM_LIMIT_BYTES),
        name="moba",
    )(q, k, vt, kmean)


ROUTER_GROUP_LANE0 = N_EXPERTS


def _mixer_out_kernel(att_ref, a_ref, gb_ref, x_ref, mod_ref, wbb_ref, wout_ref, g2_ref, wr_ref, br_ref,
                      x1_ref, h2_ref, ri_ref, rw_ref, cnt_ref, run_ref):
    tm = x_ref.shape[1]

    @pl.when((pl.program_id(0) == 0) & (pl.program_id(1) == 0))
    def _():
        run_ref[...] = jnp.zeros_like(run_ref)

    yb = jnp.dot(att_ref[0], wbb_ref[...], preferred_element_type=F32)
    merged = a_ref[0].astype(F32) + gb_ref[0].astype(F32) * yb
    mo = jnp.dot(merged.astype(BF16), wout_ref[...], preferred_element_type=F32)
    g1 = mod_ref[0, 2:3, :]
    sh2 = mod_ref[0, 3:4, :]
    sc2 = mod_ref[0, 4:5, :]
    x1 = x_ref[0] + g1 * mo
    x1_ref[0] = x1
    h2 = (_rms(x1) * g2_ref[...]) * (1.0 + sc2) + sh2
    h2_ref[0] = h2

    h_hi = h2.astype(BF16)
    h_lo = (h2 - h_hi.astype(F32)).astype(BF16)
    parts = jnp.dot(jnp.concatenate([h_hi, h_lo], axis=0), wr_ref[...], preferred_element_type=F32)
    logits = (parts[:tm, :LANES] + parts[tm:, :LANES] + parts[:tm, LANES:]) + br_ref[...]
    lane = lax.broadcasted_iota(jnp.int32, (tm, LANES), 1)
    is_group = (lane >= ROUTER_GROUP_LANE0) & (lane < ROUTER_GROUP_LANE0 + N_GROUPS)
    gl = jnp.where(is_group, logits, NEG_INF)
    ge = jnp.exp(gl - jnp.max(gl, axis=1, keepdims=True))
    g_prob = jnp.where(is_group, ge / jnp.sum(ge, axis=1, keepdims=True), -1.0)
    g_p = jnp.max(g_prob, axis=1, keepdims=True)
    lane_f = lane.astype(F32)
    g_idx = _first_index_of(g_prob == g_p, lane_f, float(LANES)) - float(ROUTER_GROUP_LANE0)
    in_group = (lane_f >= g_idx * EXPERTS_PER_GROUP) & (lane_f < (g_idx + 1.0) * EXPERTS_PER_GROUP)
    el = jnp.where(in_group, logits, NEG_INF)
    ee = jnp.exp(el - jnp.max(el, axis=1, keepdims=True))
    e_prob = jnp.where(in_group, ee / jnp.sum(ee, axis=1, keepdims=True), -1.0)
    p1 = jnp.max(e_prob, axis=1, keepdims=True)
    i1 = _first_index_of(e_prob == p1, lane_f, float(LANES))
    e_prob2 = jnp.where(lane_f == i1, -1.0, e_prob)
    p2 = jnp.max(e_prob2, axis=1, keepdims=True)
    i2 = _first_index_of(e_prob2 == p2, lane_f, float(LANES))
    w1 = g_p * p1 / (p1 + p2)
    w2 = g_p * p2 / (p1 + p2)

    hit1 = lane_f == i1
    hit2 = lane_f == i2
    onehot = jnp.where(hit1 | hit2, 1.0, 0.0)
    r_i = lax.broadcasted_iota(jnp.int32, (tm, tm), 0)
    c_i = lax.broadcasted_iota(jnp.int32, (tm, tm), 1)
    earlier = jnp.where(r_i > c_i, 1.0, 0.0).astype(BF16)
    cum = jnp.dot(earlier, onehot.astype(BF16), preferred_element_type=F32) + run_ref[...]
    rank1 = jnp.sum(jnp.where(hit1, cum, 0.0), axis=1, keepdims=True).astype(jnp.int32)
    rank2 = jnp.sum(jnp.where(hit2, cum, 0.0), axis=1, keepdims=True).astype(jnp.int32)
    run_ref[...] = run_ref[...] + jnp.sum(onehot, axis=0, keepdims=True)
    cnt_ref[...] = run_ref[...]

    ri_ref[0] = jnp.where(lane == 0, i1.astype(jnp.int32), jnp.where(lane == 1, i2.astype(jnp.int32),
                          jnp.where(lane == 2, rank1, jnp.where(lane == 3, rank2, 0))))
    rw_ref[0] = jnp.where(lane == 0, w1, jnp.where(lane == 1, w2, 0.0))


def _mixer_out(att, a_out, gb, x, mod3, w_branch_b, w_out, g2, w_router, b_router):
    B, S, D = x.shape
    tm = TOKEN_TILE
    tok = lambda w: pl.BlockSpec((1, tm, w), lambda b, i: (b, i, 0))
    return pl.pallas_call(
        _mixer_out_kernel,
        grid=(B, S // tm),
        in_specs=[tok(ATT_WIDTH), tok(D), tok(D), tok(D),
                  pl.BlockSpec((1, N_MOD, D), lambda b, i: (b, 0, 0)),
                  _const_spec(w_branch_b.shape),
                  _const_spec(w_out.shape),
                  _const_spec((1, D)),
                  _const_spec(w_router.shape),
                  _const_spec((1, LANES))],
        out_specs=[tok(D), tok(D), tok(LANES), tok(LANES),
                   pl.BlockSpec((1, LANES), lambda b, i: (0, 0))],
        out_shape=[jax.ShapeDtypeStruct((B, S, D), F32),
                   jax.ShapeDtypeStruct((B, S, D), F32),
                   jax.ShapeDtypeStruct((B, S, LANES), jnp.int32),
                   jax.ShapeDtypeStruct((B, S, LANES), F32),
                   jax.ShapeDtypeStruct((1, LANES), F32)],
        scratch_shapes=[pltpu.VMEM((1, LANES), F32)],
        compiler_params=pltpu.CompilerParams(dimension_semantics=("arbitrary", "arbitrary"),
                                             vmem_limit_bytes=VMEM_LIMIT_BYTES),
        name="mixer_out",
    )(att, a_out, gb, x, mod3, w_branch_b, w_out, g2, w_router, b_router)


def _row_copies(src_ref, dst_ref, src_rows, dst_rows, r, sem):
    return pltpu.make_async_copy(src_ref.at[pl.ds(src_rows(r), 1), :], dst_ref.at[pl.ds(dst_rows(r), 1), :], sem)


def _dispatch_kernel(tail_ref, d1_ref, d2_ref, h_ref, xb_ref, zero_ref, sem, zero_sem):
    n = h_ref.shape[0]

    @pl.when(pl.program_id(0) == 0)
    def _():
        zero_ref[...] = jnp.zeros_like(zero_ref)
        zero_block = lambda e: pltpu.make_async_copy(
            zero_ref, xb_ref.at[pl.ds(pl.multiple_of(tail_ref[e], DISPATCH_BLOCK), DISPATCH_BLOCK), :], zero_sem)
        for phase in ("start", "wait"):
            for e in range(tail_ref.shape[0]):
                @pl.when(tail_ref[e] >= 0)
                def _(e=e, phase=phase):
                    getattr(zero_block(e), phase)()

    same = lambda r: r
    copies = [functools.partial(_row_copies, h_ref, xb_ref, same, lambda r, d=d: d[0, 0, r])
              for d in (d1_ref, d2_ref)]

    def start(r, carry):
        for cp in copies:
            cp(r, sem).start()
        return carry

    lax.fori_loop(0, n, start, 0, unroll=ROW_DMA_UNROLL)
    for _ in copies:
        pltpu.make_async_copy(h_ref, xb_ref.at[pl.ds(0, n), :], sem).wait()


def _dispatch(h2, dest1, dest2, tail_block_start, n_rows):
    T, D = h2.shape
    n = ROW_DMA_TILE
    idx = pl.BlockSpec((1, 1, n), lambda i, tail: (i, 0, 0), memory_space=pltpu.SMEM)
    grid_spec = pltpu.PrefetchScalarGridSpec(
        num_scalar_prefetch=1,
        grid=(T // n,),
        in_specs=[idx, idx, pl.BlockSpec((n, D), lambda i, tail: (i, 0))],
        out_specs=pl.BlockSpec(memory_space=pl.ANY),
        scratch_shapes=[pltpu.VMEM((DISPATCH_BLOCK, D), F32), pltpu.SemaphoreType.DMA(()),
                        pltpu.SemaphoreType.DMA(())],
    )
    return pl.pallas_call(
        _dispatch_kernel,
        grid_spec=grid_spec,
        out_shape=jax.ShapeDtypeStruct((n_rows, D), F32),
        compiler_params=pltpu.CompilerParams(dimension_semantics=("arbitrary",)),
        name="dispatch",
    )(tail_block_start, dest1.reshape(T // n, 1, n), dest2.reshape(T // n, 1, n), h2)


def _expert_kernel(be_ref, nu_ref, x_ref, wg_ref, wu_ref, wd_ref, y_ref, wg_bf, wu_bf, wd_bf):
    i = pl.program_id(0)
    used = i < nu_ref[0]

    @pl.when(used & ((i == 0) | (be_ref[i] != be_ref[jnp.maximum(i - 1, 0)])))
    def _():
        wg_bf[...] = wg_ref[0].astype(BF16)
        wu_bf[...] = wu_ref[0].astype(BF16)
        wd_bf[...] = wd_ref[0].astype(BF16)

    @pl.when(used)
    def _():
        xb = x_ref[...].astype(BF16)
        gate = jnp.dot(xb, wg_bf[...], preferred_element_type=F32)
        up = jnp.dot(xb, wu_bf[...], preferred_element_type=F32)
        hid = (jax.nn.silu(gate) * up).astype(BF16)
        y_ref[...] = jnp.dot(hid, wd_bf[...], preferred_element_type=F32)

    @pl.when(i >= nu_ref[0])
    def _():
        y_ref[...] = jnp.zeros_like(y_ref)


def _experts(x_buf, block_expert, n_used, w_gate, w_up, w_down):
    P, D = x_buf.shape
    n_blocks = P // DISPATCH_BLOCK
    grid_spec = pltpu.PrefetchScalarGridSpec(
        num_scalar_prefetch=2,
        grid=(n_blocks,),
        in_specs=[pl.BlockSpec((DISPATCH_BLOCK, D), lambda i, be, nu: (jnp.minimum(i, nu[0] - 1), 0)),
                  pl.BlockSpec((1, D, D_EXPERT), lambda i, be, nu: (be[i], 0, 0)),
                  pl.BlockSpec((1, D, D_EXPERT), lambda i, be, nu: (be[i], 0, 0)),
                  pl.BlockSpec((1, D_EXPERT, D), lambda i, be, nu: (be[i], 0, 0))],
        out_specs=pl.BlockSpec((DISPATCH_BLOCK, D), lambda i, be, nu: (i, 0)),
        scratch_shapes=[pltpu.VMEM((D, D_EXPERT), BF16), pltpu.VMEM((D, D_EXPERT), BF16),
                        pltpu.VMEM((D_EXPERT, D), BF16)],
    )
    return pl.pallas_call(
        _expert_kernel,
        grid_spec=grid_spec,
        out_shape=jax.ShapeDtypeStruct((P, D), F32),
        compiler_params=pltpu.CompilerParams(dimension_semantics=("arbitrary",),
                                             vmem_limit_bytes=VMEM_LIMIT_BYTES),
        name="experts",
    )(block_expert, n_used, x_buf, w_gate, w_up, w_down)


def _combine_kernel(d1_ref, d2_ref, d1n_ref, d2n_ref, x1_ref, rw_ref, mod_ref, fg_ref, yb_ref, o_ref, y_ref, sem):
    n = x1_ref.shape[1]
    step = pl.program_id(0) * pl.num_programs(1) + pl.program_id(1)
    last = pl.num_programs(0) * pl.num_programs(1) - 1
    slot = step % 2

    def gather(dests, buf):
        def start(r, carry):
            for k, d in enumerate(dests):
                pltpu.make_async_copy(yb_ref.at[pl.ds(d[0, 0, r], 1), :], y_ref.at[buf, k, pl.ds(r, 1), :],
                                      sem.at[buf]).start()
            return carry
        lax.fori_loop(0, n, start, 0, unroll=ROW_DMA_UNROLL)

    @pl.when(step == 0)
    def _():
        gather((d1_ref, d2_ref), 0)

    @pl.when(step < last)
    def _():
        gather((d1n_ref, d2n_ref), 1 - slot)

    for k in range(TOP_K_EXPERT):
        pltpu.make_async_copy(yb_ref.at[pl.ds(0, n), :], y_ref.at[slot, k], sem.at[slot]).wait()

    rw = rw_ref[0]
    moe = rw[:, 0:1] * y_ref[slot, 0] + rw[:, 1:2] * y_ref[slot, 1]
    g2 = mod_ref[0, 5:6, :]
    o_ref[0] = _rms(x1_ref[0] + g2 * moe) * fg_ref[...]


def _combine(x1, route_w, mod3, final_g, y_buf, dest1, dest2):
    B, S, D = x1.shape
    n = ROW_DMA_TILE
    nt = S // n
    idx = pl.BlockSpec((1, 1, n), lambda b, i: (b * nt + i, 0, 0), memory_space=pltpu.SMEM)
    idx_next = pl.BlockSpec((1, 1, n), lambda b, i: (jnp.minimum(b * nt + i + 1, B * nt - 1), 0, 0),
                            memory_space=pltpu.SMEM)
    tok = lambda w: pl.BlockSpec((1, n, w), lambda b, i: (b, i, 0))
    d1 = dest1.reshape(B * nt, 1, n)
    d2 = dest2.reshape(B * nt, 1, n)
    return pl.pallas_call(
        _combine_kernel,
        grid=(B, nt),
        in_specs=[idx, idx, idx_next, idx_next, tok(D), tok(LANES),
                  pl.BlockSpec((1, N_MOD, D), lambda b, i: (b, 0, 0)),
                  pl.BlockSpec((1, D), lambda b, i: (0, 0)),
                  pl.BlockSpec(memory_space=pl.ANY)],
        out_specs=tok(D),
        out_shape=jax.ShapeDtypeStruct((B, S, D), F32),
        scratch_shapes=[pltpu.VMEM((2, TOP_K_EXPERT, n, D), F32), pltpu.SemaphoreType.DMA((2,))],
        compiler_params=pltpu.CompilerParams(dimension_semantics=("arbitrary", "arbitrary")),
        name="combine",
    )(d1, d2, d1, d2, x1, route_w, mod3, final_g, y_buf)


def _rope_tables(S):
    half = ROPE_DIM // 2
    inv_freq = jnp.power(ROPE_THETA, -jnp.arange(half, dtype=jnp.float32) * 2.0 / ROPE_DIM)
    ang = jnp.arange(S).astype(jnp.float32)[:, None] * inv_freq[None, :]
    cos = jnp.cos(ang)
    sin = jnp.sin(ang)
    rest = HEAD_DIM - ROPE_DIM
    ones = jnp.ones((S, rest), F32)
    zeros = jnp.zeros((S, rest), F32)
    zh = jnp.zeros((S, half), F32)
    per_head = lambda parts: jnp.tile(jnp.concatenate(parts, axis=1), (1, HEADS_PER_LANE_TILE))
    return per_head([cos, cos, ones]), per_head([-sin, zh, zeros]), per_head([zh, sin, zeros])


def kernel(x, c, ada_w, ada_b, norm1_g, norm2_g, w_in, gmlp_ln_g, gmlp_ln_b, w_spatial, b_spatial, w_branch_a,
           w_branch_b, w_out, w_router_group, b_router_group, w_router_expert, b_router_expert, w_gate, w_up,
           w_down, final_norm_g):
    B, S, D = x.shape
    T = B * S
    assert D == D_MODEL and S % TOKEN_TILE == 0 and S % MOBA_BLOCK == 0 and S // MOBA_BLOCK <= 32
    assert ada_w.shape[0] == 1, "single layer"
    cos_t, sin_a, sin_b = _rope_tables(S)

    mod3 = _ada_mod(c, ada_w[0], ada_b).reshape(B, N_MOD, D)

    bias_sp = jnp.repeat(b_spatial[0].T, GMLP_CH, axis=1)
    q, k, v, a_out, gb, kmean = _mixer_in(
        x, mod3, norm1_g, w_in[0].astype(BF16), w_spatial[0], bias_sp, gmlp_ln_g, gmlp_ln_b,
        w_branch_a[0].astype(BF16), cos_t, sin_a, sin_b)
    att = _moba(q, k, v, kmean.reshape(B, S // MOBA_BLOCK, ATT_WIDTH))

    w_router = jnp.zeros((D, LANES), F32)
    w_router = w_router.at[:, :N_EXPERTS].set(w_router_expert[0].reshape(D, N_EXPERTS))
    w_router = w_router.at[:, ROUTER_GROUP_LANE0:ROUTER_GROUP_LANE0 + N_GROUPS].set(w_router_group[0])
    w_router_hi = w_router.astype(BF16)
    w_router = jnp.concatenate([w_router_hi, (w_router - w_router_hi.astype(F32)).astype(BF16)], axis=1)
    b_router = jnp.zeros((1, LANES), F32)
    b_router = b_router.at[0, :N_EXPERTS].set(b_router_expert[0].reshape(N_EXPERTS))
    b_router = b_router.at[0, ROUTER_GROUP_LANE0:ROUTER_GROUP_LANE0 + N_GROUPS].set(b_router_group[0])
    x1, h2, route_i, route_w, counts = _mixer_out(
        att, a_out, gb, x, mod3, w_branch_b[0].astype(BF16), w_out[0].astype(BF16), norm2_g, w_router, b_router)

    counts = counts[0, :N_EXPERTS].astype(jnp.int32)
    padded = (counts + DISPATCH_BLOCK - 1) // DISPATCH_BLOCK * DISPATCH_BLOCK
    pend = jnp.cumsum(padded)
    pstart = pend - padded
    n_blocks = -(-(T * TOP_K_EXPERT) // DISPATCH_BLOCK) + N_EXPERTS
    block_start = jnp.arange(n_blocks, dtype=jnp.int32) * DISPATCH_BLOCK
    block_expert = jnp.minimum(
        jnp.sum((pend[None, :] <= block_start[:, None]).astype(jnp.int32), axis=1), N_EXPERTS - 1)
    n_used = (pend[-1:] // DISPATCH_BLOCK).astype(jnp.int32)
    route_i = route_i.reshape(T, LANES)
    expert_ids = jnp.arange(N_EXPERTS, dtype=jnp.int32)

    def slot(expert, rank):
        return jnp.sum(jnp.where(expert[:, None] == expert_ids[None, :], pstart[None, :], 0), axis=1) + rank

    dest1 = slot(route_i[:, 0], route_i[:, 2])
    dest2 = slot(route_i[:, 1], route_i[:, 3])

    spare = (n_used[0] + jnp.arange(N_EXPERTS, dtype=jnp.int32)) * DISPATCH_BLOCK
    tail_block_start = jnp.concatenate([
        jnp.where(counts > 0, pend - DISPATCH_BLOCK, -1),
        jnp.where(spare < n_blocks * DISPATCH_BLOCK, spare, -1)]).astype(jnp.int32)
    x_buf = _dispatch(h2.reshape(T, D), dest1, dest2, tail_block_start, n_blocks * DISPATCH_BLOCK)
    y_buf = _experts(x_buf, block_expert, n_used, w_gate[0], w_up[0], w_down[0])
    return _combine(x1, route_w, mod3, final_norm_g.reshape(1, D), y_buf, dest1, dest2)
```

```python
import functools

import jax
import jax.numpy as jnp
from jax import lax
from jax.experimental import pallas as pl
from jax.experimental.pallas import tpu as pltpu

D_MODEL = 1024
GMLP_WIDTH = D_MODEL // 2
GMLP_GROUPS = 4
GMLP_CH = GMLP_WIDTH // GMLP_GROUPS
GMLP_CHUNK = 128
ATT_HEADS = 8
HEAD_DIM = 64
ATT_WIDTH = ATT_HEADS * HEAD_DIM
ROPE_DIM = HEAD_DIM // 4
ROPE_THETA = 500000.0
MOBA_BLOCK = 256
MOBA_TOPK = 3
N_GROUPS = 4
EXPERTS_PER_GROUP = 8
N_EXPERTS = N_GROUPS * EXPERTS_PER_GROUP
TOP_K_EXPERT = 2
D_EXPERT = D_MODEL // 2
DISPATCH_BLOCK = 256
N_MOD = 6
EPS = 1e-6

LANES = 128
HEADS_PER_LANE_TILE = LANES // HEAD_DIM
VMEM_LIMIT_BYTES = 56 * 1024 * 1024

TOKEN_TILE = 512
ROW_DMA_TILE = 256
ROW_DMA_UNROLL = 8

F32 = jnp.float32
BF16 = jnp.bfloat16
NEG_INF = float("-inf")
LOG2_E = 1.4426950408889634


def _rms(x):
    return x * lax.rsqrt(jnp.mean(x * x, axis=-1, keepdims=True) + EPS)


def _first_index_of(mask, idx, sentinel):
    return jnp.min(jnp.where(mask, idx, sentinel), axis=1, keepdims=True)


def _ada_kernel(c_ref, w_ref, b_ref, o_ref):
    o_ref[...] = jnp.dot(c_ref[...], w_ref[...], preferred_element_type=F32,
                         precision=lax.Precision.HIGHEST) + b_ref[...]


def _ada_mod(c, ada_w, ada_b):
    B = c.shape[0]
    return pl.pallas_call(
        _ada_kernel,
        grid=(N_MOD,),
        in_specs=[pl.BlockSpec((B, D_MODEL), lambda j: (0, 0)),
                  pl.BlockSpec((D_MODEL, D_MODEL), lambda j: (0, j)),
                  pl.BlockSpec((1, D_MODEL), lambda j: (0, j))],
        out_specs=pl.BlockSpec((B, D_MODEL), lambda j: (0, j)),
        out_shape=jax.ShapeDtypeStruct((B, N_MOD * D_MODEL), F32),
        name="ada_mod",
    )(c, ada_w, ada_b)


def _mixer_in_kernel(x_ref, mod_ref, g1_ref, w_in_ref, wsp_ref, bsp_ref, lng_ref, lnb_ref, wba_ref,
                     cos_ref, sa_ref, sb_ref,
                     q_ref, k_ref, v_ref, a_ref, gb_ref, km_ref):
    tm = x_ref.shape[1]
    sh1 = mod_ref[0, 0:1, :]
    sc1 = mod_ref[0, 1:2, :]
    h = (_rms(x_ref[0]) * g1_ref[...]) * (1.0 + sc1) + sh1
    hb = h.astype(BF16)

    z = jax.nn.gelu(jnp.dot(hb, w_in_ref[:, 0:2 * GMLP_WIDTH], preferred_element_type=F32))
    u = z[:, :GMLP_WIDTH]
    v = z[:, GMLP_WIDTH:]
    mu = jnp.mean(v, axis=-1, keepdims=True)
    var = jnp.mean(jnp.square(v - mu), axis=-1, keepdims=True)
    vn = ((v - mu) * lax.rsqrt(var + EPS) * lng_ref[...] + lnb_ref[...]).astype(BF16)
    t_idx = lax.broadcasted_iota(jnp.int32, (GMLP_CHUNK, GMLP_CHUNK), 0)
    s_idx = lax.broadcasted_iota(jnp.int32, (GMLP_CHUNK, GMLP_CHUNK), 1)
    w_causal = [jnp.where(t_idx >= s_idx, wsp_ref[g], 0.0).astype(BF16) for g in range(GMLP_GROUPS)]
    chunks = []
    for c in range(tm // GMLP_CHUNK):
        rows = slice(c * GMLP_CHUNK, (c + 1) * GMLP_CHUNK)
        cols = [jnp.dot(w_causal[g], vn[rows, g * GMLP_CH:(g + 1) * GMLP_CH], preferred_element_type=F32)
                for g in range(GMLP_GROUPS)]
        chunks.append(jnp.concatenate(cols, axis=1) + bsp_ref[...])
    sv = jnp.concatenate(chunks, axis=0)
    ya = jnp.dot((u * sv).astype(BF16), wba_ref[...], preferred_element_type=F32)

    off = 2 * GMLP_WIDTH + 3 * ATT_WIDTH
    ga = jnp.dot(hb, w_in_ref[:, off:off + D_MODEL], preferred_element_type=F32)
    a_ref[0] = (jax.nn.sigmoid(ga) * ya).astype(BF16)
    gbv = jnp.dot(hb, w_in_ref[:, off + D_MODEL:off + 2 * D_MODEL], preferred_element_type=F32)
    gb_ref[0] = jax.nn.sigmoid(gbv).astype(BF16)

    qkv = jnp.dot(hb, w_in_ref[:, 2 * GMLP_WIDTH:2 * GMLP_WIDTH + 3 * ATT_WIDTH], preferred_element_type=F32)
    cos_t = cos_ref[...]
    sin_a = sa_ref[...]
    sin_b = sb_ref[...]
    half = ROPE_DIM // 2

    def rope(t):
        outs = []
        for j in range(ATT_WIDTH // LANES):
            tj = t[:, j * LANES:(j + 1) * LANES]
            outs.append(tj * cos_t + pltpu.roll(tj, LANES - half, 1) * sin_a + pltpu.roll(tj, half, 1) * sin_b)
        return jnp.concatenate(outs, axis=1)

    q = rope(qkv[:, :ATT_WIDTH]) * (HEAD_DIM ** -0.5 * LOG2_E)
    k = rope(qkv[:, ATT_WIDTH:2 * ATT_WIDTH])
    q_ref[0] = q.astype(BF16)
    k_ref[0] = k.astype(BF16)
    v_ref[0] = qkv[:, 2 * ATT_WIDTH:].T.astype(BF16)
    for r in range(tm // MOBA_BLOCK):
        km_ref[0, 0, r:r + 1, :] = jnp.mean(k[r * MOBA_BLOCK:(r + 1) * MOBA_BLOCK], axis=0, keepdims=True)


def _const_spec(shape):
    nd = len(shape)
    return pl.BlockSpec(shape, lambda *_: (0,) * nd, pipeline_mode=pl.Buffered(1))


def _mixer_in(x, mod3, g1, w_in, w_spatial, bias_sp, ln_g, ln_b, w_branch_a, cos_t, sin_a, sin_b):
    B, S, D = x.shape
    tm = TOKEN_TILE
    nt = S // tm
    tok = lambda w: pl.BlockSpec((1, tm, w), lambda b, i: (b, i, 0))
    tab = pl.BlockSpec((tm, LANES), lambda b, i: (i, 0))
    return pl.pallas_call(
        _mixer_in_kernel,
        grid=(B, nt),
        in_specs=[tok(D),
                  pl.BlockSpec((1, N_MOD, D), lambda b, i: (b, 0, 0)),
                  _const_spec((1, D)),
                  _const_spec(w_in.shape),
                  _const_spec(w_spatial.shape),
                  _const_spec(bias_sp.shape),
                  _const_spec((1, GMLP_WIDTH)),
                  _const_spec((1, GMLP_WIDTH)),
                  _const_spec(w_branch_a.shape),
                  tab, tab, tab],
        out_specs=[tok(ATT_WIDTH), tok(ATT_WIDTH),
                   pl.BlockSpec((1, ATT_WIDTH, tm), lambda b, i: (b, 0, i)), tok(D), tok(D),
                   pl.BlockSpec((1, 1, tm // MOBA_BLOCK, ATT_WIDTH), lambda b, i: (b, i, 0, 0))],
        out_shape=[jax.ShapeDtypeStruct((B, S, ATT_WIDTH), BF16)] * 2
                  + [jax.ShapeDtypeStruct((B, ATT_WIDTH, S), BF16)]
                  + [jax.ShapeDtypeStruct((B, S, D), BF16)] * 2
                  + [jax.ShapeDtypeStruct((B, nt, tm // MOBA_BLOCK, ATT_WIDTH), F32)],
        compiler_params=pltpu.CompilerParams(dimension_semantics=("arbitrary", "arbitrary"),
                                             vmem_limit_bytes=VMEM_LIMIT_BYTES),
        name="mixer_in",
    )(x, mod3, g1, w_in, w_spatial, bias_sp, ln_g, ln_b, w_branch_a, cos_t, sin_a, sin_b)


MOBA_DENOM_ROWS = 16


def _moba_kernel(q_ref, k_ref, vt_ref, km_ref, o_ref, *scratch):
    tq = MOBA_BLOCK

    def tile(i, carry):
        rows = pl.ds(pl.multiple_of(i * tq, tq), tq)
        o_ref[0, rows, :] = _moba_tile(i, q_ref[0, rows, :], k_ref, vt_ref, km_ref, *scratch)
        return carry

    lax.fori_loop(0, q_ref.shape[1] // tq, tile, 0)


def _moba_tile(i, q2, k_ref, vt_ref, km_ref, m_ref, acc_ref, bits_ref,
               s0_ref, s1_ref, p0_ref, p1_ref, a0_ref, a1_ref):
    tq = q2.shape[0]
    nb = km_ref.shape[1]
    n_cols = HEADS_PER_LANE_TILE * tq
    q_t = q2.astype(F32).T
    feat = lax.broadcasted_iota(jnp.int32, (LANES, tq), 0)
    qs_t = jnp.concatenate([jnp.where(feat < HEAD_DIM, q_t, 0.0), jnp.where(feat >= HEAD_DIM, q_t, 0.0)],
                           axis=1).astype(BF16)

    km = km_ref[0]
    km_hi = km.astype(BF16)
    km_lo = (km - km_hi.astype(F32)).astype(BF16)
    gate = (jnp.dot(km_hi, qs_t, preferred_element_type=F32)
            + jnp.dot(km_lo, qs_t, preferred_element_type=F32))
    n_idx = lax.broadcasted_iota(jnp.int32, (nb, n_cols), 0)
    g = jnp.where(n_idx < i, gate, NEG_INF)
    bits = jnp.zeros((1, n_cols), jnp.int32)
    for r in range(MOBA_TOPK):
        mx = jnp.max(g, axis=0, keepdims=True)
        first = jnp.min(jnp.where(g == mx, n_idx, nb), axis=0, keepdims=True)
        bits = bits | jnp.where(r < i, jnp.left_shift(1, first), 0)
        g = jnp.where(n_idx == first, NEG_INF, g)
    bits_ref[...] = bits

    def block_start(t):
        return pl.multiple_of(jnp.where(t == 0, i, t - 1) * MOBA_BLOCK, MOBA_BLOCK)

    def scores(t, s_ref):
        t = jnp.minimum(t, i)
        s_ref[...] = jnp.dot(k_ref[0, pl.ds(block_start(t), MOBA_BLOCK), :], qs_t, preferred_element_type=F32)

    def softmax_past(t, s_ref, p_ref, alpha_ref):
        j = jnp.minimum(t - 1, nb - 1)
        for c in range(n_cols // LANES):
            cols = slice(c * LANES, (c + 1) * LANES)
            s = s_ref[:, cols]
            picked = (lax.shift_right_logical(bits_ref[:, cols], j) & 1) == 1
            m_old = m_ref[:, cols]
            m_new = jnp.where(picked, jnp.maximum(m_old, jnp.max(s, axis=0, keepdims=True)), m_old)
            alpha_ref[:, cols] = jnp.exp2(m_old - m_new)
            m_sub = jnp.where(picked, m_new, jnp.inf)
            p_ref[:, cols] = jnp.exp2(s - m_sub).astype(BF16)
            m_ref[:, cols] = m_new

    def accumulate(t, p_ref, alpha_ref):
        vt = vt_ref[0, :, pl.ds(block_start(t), MOBA_BLOCK)]
        ones = jnp.ones((MOBA_DENOM_ROWS, MOBA_BLOCK), BF16)
        for h in range(HEADS_PER_LANE_TILE):
            cols = slice(h * tq, (h + 1) * tq)
            v_h = jnp.concatenate([vt[h * HEAD_DIM:(h + 1) * HEAD_DIM], ones], axis=0)
            acc_ref[h] = alpha_ref[:, cols] * acc_ref[h] + jnp.dot(v_h, p_ref[:, cols], preferred_element_type=F32)

    s_refs, p_refs, alpha_refs = (s0_ref, s1_ref), (p0_ref, p1_ref), (a0_ref, a1_ref)

    scores(0, s_refs[0])
    key = lax.broadcasted_iota(jnp.int32, (MOBA_BLOCK, LANES), 0)
    qry = lax.broadcasted_iota(jnp.int32, (MOBA_BLOCK, LANES), 1)
    for c in range(n_cols // LANES):
        cols = slice(c * LANES, (c + 1) * LANES)
        s = jnp.where(key <= qry + (c * LANES) % tq, s_refs[0][:, cols], NEG_INF)
        m0 = jnp.max(s, axis=0, keepdims=True)
        p_refs[0][:, cols] = jnp.exp2(s - m0).astype(BF16)
        m_ref[:, cols] = m0
    alpha_refs[0][...] = jnp.ones_like(alpha_refs[0])
    acc_ref[...] = jnp.zeros_like(acc_ref)
    scores(1, s_refs[1])

    def tick_pair(d, carry):
        for half in range(2):
            t = 2 * d + 1 + half
            cur, prev = (1 - half), half
            accumulate(t - 1, p_refs[prev], alpha_refs[prev])
            softmax_past(t, s_refs[cur], p_refs[cur], alpha_refs[cur])
            scores(t + 1, s_refs[prev])
        return carry

    lax.fori_loop(0, i // 2, tick_pair, 0)

    @pl.when(i % 2 == 1)
    def _():
        accumulate(i - 1, p_refs[0], alpha_refs[0])
        softmax_past(i, s_refs[1], p_refs[1], alpha_refs[1])
        accumulate(i, p_refs[1], alpha_refs[1])

    @pl.when(i % 2 == 0)
    def _():
        accumulate(i, p_refs[0], alpha_refs[0])

    outs = [acc_ref[h, :HEAD_DIM, :] / acc_ref[h, HEAD_DIM:HEAD_DIM + 1, :] for h in range(HEADS_PER_LANE_TILE)]
    return jnp.concatenate(outs, axis=0).T.astype(BF16)


def _moba(q, k, vt, kmean):
    B, S, _ = q.shape
    tq = MOBA_BLOCK
    nb = S // MOBA_BLOCK
    n_pairs = ATT_HEADS // HEADS_PER_LANE_TILE
    n_cols = HEADS_PER_LANE_TILE * tq
    seq = pl.BlockSpec((1, S, LANES), lambda b, hp: (b, 0, hp))
    return pl.pallas_call(
        _moba_kernel,
        grid=(B, n_pairs),
        in_specs=[seq, seq,
                  pl.BlockSpec((1, LANES, S), lambda b, hp: (b, hp, 0)),
                  pl.BlockSpec((1, nb, LANES), lambda b, hp: (b, 0, hp))],
        out_specs=seq,
        out_shape=jax.ShapeDtypeStruct((B, S, ATT_WIDTH), BF16),
        scratch_shapes=[pltpu.VMEM((1, n_cols), F32),
                        pltpu.VMEM((HEADS_PER_LANE_TILE, HEAD_DIM + MOBA_DENOM_ROWS, tq), F32),
                        pltpu.VMEM((1, n_cols), jnp.int32),
                        pltpu.VMEM((MOBA_BLOCK, n_cols), F32),
                        pltpu.VMEM((MOBA_BLOCK, n_cols), F32),
                        pltpu.VMEM((MOBA_BLOCK, n_cols), BF16),
                        pltpu.VMEM((MOBA_BLOCK, n_cols), BF16),
                        pltpu.VMEM((1, n_cols), F32),
                        pltpu.VMEM((1, n_cols), F32)],
        compiler_params=pltpu.CompilerParams(dimension_semantics=("arbitrary", "arbitrary"),
                                             vmem_limit_bytes=VMEM_LIMIT_BYTES),
        name="moba",
    )(q, k, vt, kmean)


ROUTER_GROUP_LANE0 = N_EXPERTS
ROUTE_ROWS = 8


def _mixer_out_kernel(att_ref, a_ref, gb_ref, x_ref, mod_ref, wbb_ref, wout_ref, g2_ref, wr_ref, br_ref,
                      x1_ref, h2_ref, ri_ref, rw_ref, cnt_ref, run_ref):
    tm = x_ref.shape[1]

    @pl.when((pl.program_id(0) == 0) & (pl.program_id(1) == 0))
    def _():
        run_ref[...] = jnp.zeros_like(run_ref)

    yb = jnp.dot(att_ref[0], wbb_ref[...], preferred_element_type=F32)
    merged = a_ref[0].astype(F32) + gb_ref[0].astype(F32) * yb
    mo = jnp.dot(merged.astype(BF16), wout_ref[...], preferred_element_type=F32)
    g1 = mod_ref[0, 2:3, :]
    sh2 = mod_ref[0, 3:4, :]
    sc2 = mod_ref[0, 4:5, :]
    x1 = x_ref[0] + g1 * mo
    x1_ref[0] = x1
    h2 = (_rms(x1) * g2_ref[...]) * (1.0 + sc2) + sh2
    h2_ref[0] = h2

    h_hi = h2.astype(BF16)
    h_lo = (h2 - h_hi.astype(F32)).astype(BF16)
    parts = jnp.dot(jnp.concatenate([h_hi, h_lo], axis=0), wr_ref[...], preferred_element_type=F32)
    logits = (parts[:tm, :LANES] + parts[tm:, :LANES] + parts[:tm, LANES:]) + br_ref[...]
    lane = lax.broadcasted_iota(jnp.int32, (tm, LANES), 1)
    is_group = (lane >= ROUTER_GROUP_LANE0) & (lane < ROUTER_GROUP_LANE0 + N_GROUPS)
    gl = jnp.where(is_group, logits, NEG_INF)
    ge = jnp.exp(gl - jnp.max(gl, axis=1, keepdims=True))
    g_prob = jnp.where(is_group, ge / jnp.sum(ge, axis=1, keepdims=True), -1.0)
    g_p = jnp.max(g_prob, axis=1, keepdims=True)
    lane_f = lane.astype(F32)
    g_idx = _first_index_of(g_prob == g_p, lane_f, float(LANES)) - float(ROUTER_GROUP_LANE0)
    in_group = (lane_f >= g_idx * EXPERTS_PER_GROUP) & (lane_f < (g_idx + 1.0) * EXPERTS_PER_GROUP)
    el = jnp.where(in_group, logits, NEG_INF)
    ee = jnp.exp(el - jnp.max(el, axis=1, keepdims=True))
    e_prob = jnp.where(in_group, ee / jnp.sum(ee, axis=1, keepdims=True), -1.0)
    p1 = jnp.max(e_prob, axis=1, keepdims=True)
    i1 = _first_index_of(e_prob == p1, lane_f, float(LANES))
    e_prob2 = jnp.where(lane_f == i1, -1.0, e_prob)
    p2 = jnp.max(e_prob2, axis=1, keepdims=True)
    i2 = _first_index_of(e_prob2 == p2, lane_f, float(LANES))
    w1 = g_p * p1 / (p1 + p2)
    w2 = g_p * p2 / (p1 + p2)

    hit1 = lane_f == i1
    hit2 = lane_f == i2
    onehot = jnp.where(hit1 | hit2, 1.0, 0.0)
    r_i = lax.broadcasted_iota(jnp.int32, (tm, tm), 0)
    c_i = lax.broadcasted_iota(jnp.int32, (tm, tm), 1)
    earlier = jnp.where(r_i > c_i, 1.0, 0.0).astype(BF16)
    cum = jnp.dot(earlier, onehot.astype(BF16), preferred_element_type=F32) + run_ref[...]
    rank1 = jnp.sum(jnp.where(hit1, cum, 0.0), axis=1, keepdims=True).astype(jnp.int32)
    rank2 = jnp.sum(jnp.where(hit2, cum, 0.0), axis=1, keepdims=True).astype(jnp.int32)
    run_ref[...] = run_ref[...] + jnp.sum(onehot, axis=0, keepdims=True)
    cnt_ref[...] = run_ref[...]

    ri = jnp.where(lane == 0, i1.astype(jnp.int32), jnp.where(lane == 1, i2.astype(jnp.int32),
                   jnp.where(lane == 2, rank1, jnp.where(lane == 3, rank2, 0))))
    ri_ref[...] = ri.T[:ROUTE_ROWS, :]
    rw_ref[0] = jnp.where(lane == 0, w1, jnp.where(lane == 1, w2, 0.0))


def _mixer_out(att, a_out, gb, x, mod3, w_branch_b, w_out, g2, w_router, b_router):
    B, S, D = x.shape
    tm = TOKEN_TILE
    tok = lambda w: pl.BlockSpec((1, tm, w), lambda b, i: (b, i, 0))
    return pl.pallas_call(
        _mixer_out_kernel,
        grid=(B, S // tm),
        in_specs=[tok(ATT_WIDTH), tok(D), tok(D), tok(D),
                  pl.BlockSpec((1, N_MOD, D), lambda b, i: (b, 0, 0)),
                  _const_spec(w_branch_b.shape),
                  _const_spec(w_out.shape),
                  _const_spec((1, D)),
                  _const_spec(w_router.shape),
                  _const_spec((1, LANES))],
        out_specs=[tok(D), tok(D),
                   pl.BlockSpec((ROUTE_ROWS, tm), lambda b, i: (0, b * (S // tm) + i)), tok(LANES),
                   pl.BlockSpec((1, LANES), lambda b, i: (0, 0))],
        out_shape=[jax.ShapeDtypeStruct((B, S, D), F32),
                   jax.ShapeDtypeStruct((B, S, D), F32),
                   jax.ShapeDtypeStruct((ROUTE_ROWS, B * S), jnp.int32),
                   jax.ShapeDtypeStruct((B, S, LANES), F32),
                   jax.ShapeDtypeStruct((1, LANES), F32)],
        scratch_shapes=[pltpu.VMEM((1, LANES), F32)],
        compiler_params=pltpu.CompilerParams(dimension_semantics=("arbitrary", "arbitrary"),
                                             vmem_limit_bytes=VMEM_LIMIT_BYTES),
        name="mixer_out",
    )(att, a_out, gb, x, mod3, w_branch_b, w_out, g2, w_router, b_router)


def _row_copies(src_ref, dst_ref, src_rows, dst_rows, r, sem):
    return pltpu.make_async_copy(src_ref.at[pl.ds(src_rows(r), 1), :], dst_ref.at[pl.ds(dst_rows(r), 1), :], sem)


def _dispatch_kernel(tail_ref, d1_ref, d2_ref, h_ref, xb_ref, zero_ref, sem, zero_sem):
    n = h_ref.shape[0]

    @pl.when(pl.program_id(0) == 0)
    def _():
        zero_ref[...] = jnp.zeros_like(zero_ref)
        zero_block = lambda e: pltpu.make_async_copy(
            zero_ref, xb_ref.at[pl.ds(pl.multiple_of(tail_ref[e], DISPATCH_BLOCK), DISPATCH_BLOCK), :], zero_sem)
        for phase in ("start", "wait"):
            for e in range(tail_ref.shape[0]):
                @pl.when(tail_ref[e] >= 0)
                def _(e=e, phase=phase):
                    getattr(zero_block(e), phase)()

    same = lambda r: r
    copies = [functools.partial(_row_copies, h_ref, xb_ref, same, lambda r, d=d: d[0, 0, r])
              for d in (d1_ref, d2_ref)]

    def start(r, carry):
        for cp in copies:
            cp(r, sem).start()
        return carry

    lax.fori_loop(0, n, start, 0, unroll=ROW_DMA_UNROLL)
    for _ in copies:
        pltpu.make_async_copy(h_ref, xb_ref.at[pl.ds(0, n), :], sem).wait()


def _dispatch(h2, dest1, dest2, tail_block_start, n_rows):
    T, D = h2.shape
    n = ROW_DMA_TILE
    idx = pl.BlockSpec((1, 1, n), lambda i, tail: (i, 0, 0), memory_space=pltpu.SMEM)
    grid_spec = pltpu.PrefetchScalarGridSpec(
        num_scalar_prefetch=1,
        grid=(T // n,),
        in_specs=[idx, idx, pl.BlockSpec((n, D), lambda i, tail: (i, 0))],
        out_specs=pl.BlockSpec(memory_space=pl.ANY),
        scratch_shapes=[pltpu.VMEM((DISPATCH_BLOCK, D), F32), pltpu.SemaphoreType.DMA(()),
                        pltpu.SemaphoreType.DMA(())],
    )
    return pl.pallas_call(
        _dispatch_kernel,
        grid_spec=grid_spec,
        out_shape=jax.ShapeDtypeStruct((n_rows, D), F32),
        compiler_params=pltpu.CompilerParams(dimension_semantics=("arbitrary",)),
        name="dispatch",
    )(tail_block_start, dest1.reshape(T // n, 1, n), dest2.reshape(T // n, 1, n), h2)


def _expert_kernel(be_ref, nu_ref, x_ref, wg_ref, wu_ref, wd_ref, y_ref, wg_bf, wu_bf, wd_bf):
    i = pl.program_id(0)
    used = i < nu_ref[0]

    @pl.when(used & ((i == 0) | (be_ref[i] != be_ref[jnp.maximum(i - 1, 0)])))
    def _():
        wg_bf[...] = wg_ref[0].astype(BF16)
        wu_bf[...] = wu_ref[0].astype(BF16)
        wd_bf[...] = wd_ref[0].astype(BF16)

    @pl.when(used)
    def _():
        xb = x_ref[...].astype(BF16)
        gate = jnp.dot(xb, wg_bf[...], preferred_element_type=F32)
        up = jnp.dot(xb, wu_bf[...], preferred_element_type=F32)
        hid = (jax.nn.silu(gate) * up).astype(BF16)
        y_ref[...] = jnp.dot(hid, wd_bf[...], preferred_element_type=F32)

    @pl.when(i >= nu_ref[0])
    def _():
        y_ref[...] = jnp.zeros_like(y_ref)


def _experts(x_buf, block_expert, n_used, w_gate, w_up, w_down):
    P, D = x_buf.shape
    n_blocks = P // DISPATCH_BLOCK
    grid_spec = pltpu.PrefetchScalarGridSpec(
        num_scalar_prefetch=2,
        grid=(n_blocks,),
        in_specs=[pl.BlockSpec((DISPATCH_BLOCK, D), lambda i, be, nu: (jnp.minimum(i, nu[0] - 1), 0)),
                  pl.BlockSpec((1, D, D_EXPERT), lambda i, be, nu: (be[i], 0, 0)),
                  pl.BlockSpec((1, D, D_EXPERT), lambda i, be, nu: (be[i], 0, 0)),
                  pl.BlockSpec((1, D_EXPERT, D), lambda i, be, nu: (be[i], 0, 0))],
        out_specs=pl.BlockSpec((DISPATCH_BLOCK, D), lambda i, be, nu: (i, 0)),
        scratch_shapes=[pltpu.VMEM((D, D_EXPERT), BF16), pltpu.VMEM((D, D_EXPERT), BF16),
                        pltpu.VMEM((D_EXPERT, D), BF16)],
    )
    return pl.pallas_call(
        _expert_kernel,
        grid_spec=grid_spec,
        out_shape=jax.ShapeDtypeStruct((P, D), F32),
        compiler_params=pltpu.CompilerParams(dimension_semantics=("arbitrary",),
                                             vmem_limit_bytes=VMEM_LIMIT_BYTES),
        name="experts",
    )(block_expert, n_used, x_buf, w_gate, w_up, w_down)


def _combine_kernel(d1_ref, d2_ref, d1n_ref, d2n_ref, x1_ref, rw_ref, mod_ref, fg_ref, yb_ref, o_ref, y_ref, sem):
    n = x1_ref.shape[1]
    step = pl.program_id(0) * pl.num_programs(1) + pl.program_id(1)
    last = pl.num_programs(0) * pl.num_programs(1) - 1
    slot = step % 2

    def gather(dests, buf):
        def start(r, carry):
            for k, d in enumerate(dests):
                pltpu.make_async_copy(yb_ref.at[pl.ds(d[0, 0, r], 1), :], y_ref.at[buf, k, pl.ds(r, 1), :],
                                      sem.at[buf]).start()
            return carry
        lax.fori_loop(0, n, start, 0, unroll=ROW_DMA_UNROLL)

    @pl.when(step == 0)
    def _():
        gather((d1_ref, d2_ref), 0)

    @pl.when(step < last)
    def _():
        gather((d1n_ref, d2n_ref), 1 - slot)

    for k in range(TOP_K_EXPERT):
        pltpu.make_async_copy(yb_ref.at[pl.ds(0, n), :], y_ref.at[slot, k], sem.at[slot]).wait()

    rw = rw_ref[0]
    moe = rw[:, 0:1] * y_ref[slot, 0] + rw[:, 1:2] * y_ref[slot, 1]
    g2 = mod_ref[0, 5:6, :]
    o_ref[0] = _rms(x1_ref[0] + g2 * moe) * fg_ref[...]


def _combine(x1, route_w, mod3, final_g, y_buf, dest1, dest2):
    B, S, D = x1.shape
    n = ROW_DMA_TILE
    nt = S // n
    idx = pl.BlockSpec((1, 1, n), lambda b, i: (b * nt + i, 0, 0), memory_space=pltpu.SMEM)
    idx_next = pl.BlockSpec((1, 1, n), lambda b, i: (jnp.minimum(b * nt + i + 1, B * nt - 1), 0, 0),
                            memory_space=pltpu.SMEM)
    tok = lambda w: pl.BlockSpec((1, n, w), lambda b, i: (b, i, 0))
    d1 = dest1.reshape(B * nt, 1, n)
    d2 = dest2.reshape(B * nt, 1, n)
    return pl.pallas_call(
        _combine_kernel,
        grid=(B, nt),
        in_specs=[idx, idx, idx_next, idx_next, tok(D), tok(LANES),
                  pl.BlockSpec((1, N_MOD, D), lambda b, i: (b, 0, 0)),
                  pl.BlockSpec((1, D), lambda b, i: (0, 0)),
                  pl.BlockSpec(memory_space=pl.ANY)],
        out_specs=tok(D),
        out_shape=jax.ShapeDtypeStruct((B, S, D), F32),
        scratch_shapes=[pltpu.VMEM((2, TOP_K_EXPERT, n, D), F32), pltpu.SemaphoreType.DMA((2,))],
        compiler_params=pltpu.CompilerParams(dimension_semantics=("arbitrary", "arbitrary")),
        name="combine",
    )(d1, d2, d1, d2, x1, route_w, mod3, final_g, y_buf)


def _rope_tables(S):
    half = ROPE_DIM // 2
    inv_freq = jnp.power(ROPE_THETA, -jnp.arange(half, dtype=jnp.float32) * 2.0 / ROPE_DIM)
    ang = jnp.arange(S).astype(jnp.float32)[:, None] * inv_freq[None, :]
    cos = jnp.cos(ang)
    sin = jnp.sin(ang)
    rest = HEAD_DIM - ROPE_DIM
    ones = jnp.ones((S, rest), F32)
    zeros = jnp.zeros((S, rest), F32)
    zh = jnp.zeros((S, half), F32)
    per_head = lambda parts: jnp.tile(jnp.concatenate(parts, axis=1), (1, HEADS_PER_LANE_TILE))
    return per_head([cos, cos, ones]), per_head([-sin, zh, zeros]), per_head([zh, sin, zeros])


def kernel(x, c, ada_w, ada_b, norm1_g, norm2_g, w_in, gmlp_ln_g, gmlp_ln_b, w_spatial, b_spatial, w_branch_a,
           w_branch_b, w_out, w_router_group, b_router_group, w_router_expert, b_router_expert, w_gate, w_up,
           w_down, final_norm_g):
    B, S, D = x.shape
    T = B * S
    assert D == D_MODEL and S % TOKEN_TILE == 0 and S % MOBA_BLOCK == 0 and S // MOBA_BLOCK <= 32
    assert ada_w.shape[0] == 1, "single layer"
    cos_t, sin_a, sin_b = _rope_tables(S)

    mod3 = _ada_mod(c, ada_w[0], ada_b).reshape(B, N_MOD, D)

    bias_sp = jnp.repeat(b_spatial[0].T, GMLP_CH, axis=1)
    q, k, v, a_out, gb, kmean = _mixer_in(
        x, mod3, norm1_g, w_in[0].astype(BF16), w_spatial[0], bias_sp, gmlp_ln_g, gmlp_ln_b,
        w_branch_a[0].astype(BF16), cos_t, sin_a, sin_b)
    att = _moba(q, k, v, kmean.reshape(B, S // MOBA_BLOCK, ATT_WIDTH))

    w_router = jnp.zeros((D, LANES), F32)
    w_router = w_router.at[:, :N_EXPERTS].set(w_router_expert[0].reshape(D, N_EXPERTS))
    w_router = w_router.at[:, ROUTER_GROUP_LANE0:ROUTER_GROUP_LANE0 + N_GROUPS].set(w_router_group[0])
    w_router_hi = w_router.astype(BF16)
    w_router = jnp.concatenate([w_router_hi, (w_router - w_router_hi.astype(F32)).astype(BF16)], axis=1)
    b_router = jnp.zeros((1, LANES), F32)
    b_router = b_router.at[0, :N_EXPERTS].set(b_router_expert[0].reshape(N_EXPERTS))
    b_router = b_router.at[0, ROUTER_GROUP_LANE0:ROUTER_GROUP_LANE0 + N_GROUPS].set(b_router_group[0])
    x1, h2, route_i, route_w, counts = _mixer_out(
        att, a_out, gb, x, mod3, w_branch_b[0].astype(BF16), w_out[0].astype(BF16), norm2_g, w_router, b_router)

    counts = counts[0, :N_EXPERTS].astype(jnp.int32)
    padded = (counts + DISPATCH_BLOCK - 1) // DISPATCH_BLOCK * DISPATCH_BLOCK
    pend = jnp.cumsum(padded)
    pstart = pend - padded
    n_blocks = -(-(T * TOP_K_EXPERT) // DISPATCH_BLOCK) + N_EXPERTS
    block_start = jnp.arange(n_blocks, dtype=jnp.int32) * DISPATCH_BLOCK
    block_expert = jnp.minimum(
        jnp.sum((pend[None, :] <= block_start[:, None]).astype(jnp.int32), axis=1), N_EXPERTS - 1)
    n_used = (pend[-1:] // DISPATCH_BLOCK).astype(jnp.int32)
    expert_ids = jnp.arange(N_EXPERTS, dtype=jnp.int32)

    def slot(expert, rank):
        return jnp.sum(jnp.where(expert[:, None] == expert_ids[None, :], pstart[None, :], 0), axis=1) + rank

    dest1 = slot(route_i[0], route_i[2])
    dest2 = slot(route_i[1], route_i[3])

    spare = (n_used[0] + jnp.arange(N_EXPERTS, dtype=jnp.int32)) * DISPATCH_BLOCK
    tail_block_start = jnp.concatenate([
        jnp.where(counts > 0, pend - DISPATCH_BLOCK, -1),
        jnp.where(spare < n_blocks * DISPATCH_BLOCK, spare, -1)]).astype(jnp.int32)
    x_buf = _dispatch(h2.reshape(T, D), dest1, dest2, tail_block_start, n_blocks * DISPATCH_BLOCK)
    y_buf = _experts(x_buf, block_expert, n_used, w_gate[0], w_up[0], w_down[0])
    return _combine(x1, route_w, mod3, final_norm_g.reshape(1, D), y_buf, dest1, dest2)
```

```python
import functools

import jax
import jax.numpy as jnp
from jax import lax
from jax.experimental import pallas as pl
from jax.experimental.pallas import tpu as pltpu

D_MODEL = 1024
GMLP_WIDTH = D_MODEL // 2
GMLP_GROUPS = 4
GMLP_CH = GMLP_WIDTH // GMLP_GROUPS
GMLP_CHUNK = 128
ATT_HEADS = 8
HEAD_DIM = 64
ATT_WIDTH = ATT_HEADS * HEAD_DIM
ROPE_DIM = HEAD_DIM // 4
ROPE_THETA = 500000.0
MOBA_BLOCK = 256
MOBA_TOPK = 3
N_GROUPS = 4
EXPERTS_PER_GROUP = 8
N_EXPERTS = N_GROUPS * EXPERTS_PER_GROUP
TOP_K_EXPERT = 2
D_EXPERT = D_MODEL // 2
DISPATCH_BLOCK = 256
N_MOD = 6
EPS = 1e-6

LANES = 128
HEADS_PER_LANE_TILE = LANES // HEAD_DIM
VMEM_LIMIT_BYTES = 56 * 1024 * 1024

TOKEN_TILE = 512
ROW_DMA_TILE = 256
ROW_DMA_UNROLL = 8

F32 = jnp.float32
BF16 = jnp.bfloat16
NEG_INF = float("-inf")
LOG2_E = 1.4426950408889634


def _rms(x):
    return x * lax.rsqrt(jnp.mean(x * x, axis=-1, keepdims=True) + EPS)


def _first_index_of(mask, idx, sentinel):
    return jnp.min(jnp.where(mask, idx, sentinel), axis=1, keepdims=True)


def _ada_kernel(c_ref, w_ref, b_ref, o_ref):
    o_ref[...] = jnp.dot(c_ref[...], w_ref[...], preferred_element_type=F32,
                         precision=lax.Precision.HIGHEST) + b_ref[...]


def _ada_mod(c, ada_w, ada_b):
    B = c.shape[0]
    return pl.pallas_call(
        _ada_kernel,
        grid=(N_MOD,),
        in_specs=[pl.BlockSpec((B, D_MODEL), lambda j: (0, 0)),
                  pl.BlockSpec((D_MODEL, D_MODEL), lambda j: (0, j)),
                  pl.BlockSpec((1, D_MODEL), lambda j: (0, j))],
        out_specs=pl.BlockSpec((B, D_MODEL), lambda j: (0, j)),
        out_shape=jax.ShapeDtypeStruct((B, N_MOD * D_MODEL), F32),
        name="ada_mod",
    )(c, ada_w, ada_b)


def _mixer_in_kernel(x_ref, mod_ref, g1_ref, w_in_ref, wsp_ref, bsp_ref, lng_ref, lnb_ref, wba_ref,
                     cos_ref, sa_ref, sb_ref,
                     q_ref, k_ref, v_ref, a_ref, gb_ref, km_ref):
    tm = x_ref.shape[1]
    sh1 = mod_ref[0, 0:1, :]
    sc1 = mod_ref[0, 1:2, :]
    h = (_rms(x_ref[0]) * g1_ref[...]) * (1.0 + sc1) + sh1
    hb = h.astype(BF16)

    z = jax.nn.gelu(jnp.dot(hb, w_in_ref[:, 0:2 * GMLP_WIDTH], preferred_element_type=F32))
    u = z[:, :GMLP_WIDTH]
    v = z[:, GMLP_WIDTH:]
    mu = jnp.mean(v, axis=-1, keepdims=True)
    var = jnp.mean(jnp.square(v - mu), axis=-1, keepdims=True)
    vn = ((v - mu) * lax.rsqrt(var + EPS) * lng_ref[...] + lnb_ref[...]).astype(BF16)
    t_idx = lax.broadcasted_iota(jnp.int32, (GMLP_CHUNK, GMLP_CHUNK), 0)
    s_idx = lax.broadcasted_iota(jnp.int32, (GMLP_CHUNK, GMLP_CHUNK), 1)
    w_causal = [jnp.where(t_idx >= s_idx, wsp_ref[g], 0.0).astype(BF16) for g in range(GMLP_GROUPS)]
    chunks = []
    for c in range(tm // GMLP_CHUNK):
        rows = slice(c * GMLP_CHUNK, (c + 1) * GMLP_CHUNK)
        cols = [jnp.dot(w_causal[g], vn[rows, g * GMLP_CH:(g + 1) * GMLP_CH], preferred_element_type=F32)
                for g in range(GMLP_GROUPS)]
        chunks.append(jnp.concatenate(cols, axis=1) + bsp_ref[...])
    sv = jnp.concatenate(chunks, axis=0)
    ya = jnp.dot((u * sv).astype(BF16), wba_ref[...], preferred_element_type=F32)

    off = 2 * GMLP_WIDTH + 3 * ATT_WIDTH
    ga = jnp.dot(hb, w_in_ref[:, off:off + D_MODEL], preferred_element_type=F32)
    a_ref[0] = (jax.nn.sigmoid(ga) * ya).astype(BF16)
    gbv = jnp.dot(hb, w_in_ref[:, off + D_MODEL:off + 2 * D_MODEL], preferred_element_type=F32)
    gb_ref[0] = jax.nn.sigmoid(gbv).astype(BF16)

    qkv = jnp.dot(hb, w_in_ref[:, 2 * GMLP_WIDTH:2 * GMLP_WIDTH + 3 * ATT_WIDTH], preferred_element_type=F32)
    cos_t = cos_ref[...]
    sin_a = sa_ref[...]
    sin_b = sb_ref[...]
    half = ROPE_DIM // 2

    def rope(t):
        outs = []
        for j in range(ATT_WIDTH // LANES):
            tj = t[:, j * LANES:(j + 1) * LANES]
            outs.append(tj * cos_t + pltpu.roll(tj, LANES - half, 1) * sin_a + pltpu.roll(tj, half, 1) * sin_b)
        return jnp.concatenate(outs, axis=1)

    q = rope(qkv[:, :ATT_WIDTH]) * (HEAD_DIM ** -0.5 * LOG2_E)
    k = rope(qkv[:, ATT_WIDTH:2 * ATT_WIDTH])
    q_ref[0] = q.astype(BF16)
    k_ref[0] = k.astype(BF16)
    v_ref[0] = qkv[:, 2 * ATT_WIDTH:].T.astype(BF16)
    for r in range(tm // MOBA_BLOCK):
        km_ref[0, 0, r:r + 1, :] = jnp.mean(k[r * MOBA_BLOCK:(r + 1) * MOBA_BLOCK], axis=0, keepdims=True)


def _const_spec(shape):
    nd = len(shape)
    return pl.BlockSpec(shape, lambda *_: (0,) * nd, pipeline_mode=pl.Buffered(1))


def _mixer_in(x, mod3, g1, w_in, w_spatial, bias_sp, ln_g, ln_b, w_branch_a, cos_t, sin_a, sin_b):
    B, S, D = x.shape
    tm = TOKEN_TILE
    nt = S // tm
    tok = lambda w: pl.BlockSpec((1, tm, w), lambda b, i: (b, i, 0))
    tab = pl.BlockSpec((tm, LANES), lambda b, i: (i, 0))
    return pl.pallas_call(
        _mixer_in_kernel,
        grid=(B, nt),
        in_specs=[tok(D),
                  pl.BlockSpec((1, N_MOD, D), lambda b, i: (b, 0, 0)),
                  _const_spec((1, D)),
                  _const_spec(w_in.shape),
                  _const_spec(w_spatial.shape),
                  _const_spec(bias_sp.shape),
                  _const_spec((1, GMLP_WIDTH)),
                  _const_spec((1, GMLP_WIDTH)),
                  _const_spec(w_branch_a.shape),
                  tab, tab, tab],
        out_specs=[tok(ATT_WIDTH), tok(ATT_WIDTH),
                   pl.BlockSpec((1, ATT_WIDTH, tm), lambda b, i: (b, 0, i)), tok(D), tok(D),
                   pl.BlockSpec((1, 1, tm // MOBA_BLOCK, ATT_WIDTH), lambda b, i: (b, i, 0, 0))],
        out_shape=[jax.ShapeDtypeStruct((B, S, ATT_WIDTH), BF16)] * 2
                  + [jax.ShapeDtypeStruct((B, ATT_WIDTH, S), BF16)]
                  + [jax.ShapeDtypeStruct((B, S, D), BF16)] * 2
                  + [jax.ShapeDtypeStruct((B, nt, tm // MOBA_BLOCK, ATT_WIDTH), F32)],
        compiler_params=pltpu.CompilerParams(dimension_semantics=("arbitrary", "arbitrary"),
                                             vmem_limit_bytes=VMEM_LIMIT_BYTES),
        name="mixer_in",
    )(x, mod3, g1, w_in, w_spatial, bias_sp, ln_g, ln_b, w_branch_a, cos_t, sin_a, sin_b)


MOBA_DENOM_ROWS = 16


def _moba_kernel(q_ref, k_ref, vt_ref, km_ref, o_ref, *scratch):
    tq = MOBA_BLOCK

    def tile(i, carry):
        rows = pl.ds(pl.multiple_of(i * tq, tq), tq)
        o_ref[0, rows, :] = _moba_tile(i, q_ref[0, rows, :], k_ref, vt_ref, km_ref, *scratch)
        return carry

    lax.fori_loop(0, q_ref.shape[1] // tq, tile, 0)


def _moba_tile(i, q2, k_ref, vt_ref, km_ref, m_ref, acc_ref, bits_ref,
               s0_ref, s1_ref, p0_ref, p1_ref, a0_ref, a1_ref):
    tq = q2.shape[0]
    nb = km_ref.shape[1]
    n_cols = HEADS_PER_LANE_TILE * tq
    q_t = q2.astype(F32).T
    feat = lax.broadcasted_iota(jnp.int32, (LANES, tq), 0)
    qs_t = jnp.concatenate([jnp.where(feat < HEAD_DIM, q_t, 0.0), jnp.where(feat >= HEAD_DIM, q_t, 0.0)],
                           axis=1).astype(BF16)

    km = km_ref[0]
    km_hi = km.astype(BF16)
    km_lo = (km - km_hi.astype(F32)).astype(BF16)
    gate = (jnp.dot(km_hi, qs_t, preferred_element_type=F32)
            + jnp.dot(km_lo, qs_t, preferred_element_type=F32))
    n_idx = lax.broadcasted_iota(jnp.int32, (nb, n_cols), 0)
    g = jnp.where(n_idx < i, gate, NEG_INF)
    bits = jnp.zeros((1, n_cols), jnp.int32)
    for r in range(MOBA_TOPK):
        mx = jnp.max(g, axis=0, keepdims=True)
        first = jnp.min(jnp.where(g == mx, n_idx, nb), axis=0, keepdims=True)
        bits = bits | jnp.where(r < i, jnp.left_shift(1, first), 0)
        g = jnp.where(n_idx == first, NEG_INF, g)
    bits_ref[...] = bits

    def block_start(t):
        return pl.multiple_of(jnp.where(t == 0, i, t - 1) * MOBA_BLOCK, MOBA_BLOCK)

    def scores(t, s_ref):
        t = jnp.minimum(t, i)
        s_ref[...] = jnp.dot(k_ref[0, pl.ds(block_start(t), MOBA_BLOCK), :], qs_t, preferred_element_type=F32)

    def softmax_past(t, s_ref, p_ref, alpha_ref):
        j = jnp.minimum(t - 1, nb - 1)
        for c in range(n_cols // LANES):
            cols = slice(c * LANES, (c + 1) * LANES)
            s = s_ref[:, cols]
            picked = (lax.shift_right_logical(bits_ref[:, cols], j) & 1) == 1
            m_old = m_ref[:, cols]
            m_new = jnp.where(picked, jnp.maximum(m_old, jnp.max(s, axis=0, keepdims=True)), m_old)
            alpha_ref[:, cols] = jnp.exp2(m_old - m_new)
            m_sub = jnp.where(picked, m_new, jnp.inf)
            p_ref[:, cols] = jnp.exp2(s - m_sub).astype(BF16)
            m_ref[:, cols] = m_new

    def accumulate(t, p_ref, alpha_ref):
        vt = vt_ref[0, :, pl.ds(block_start(t), MOBA_BLOCK)]
        ones = jnp.ones((MOBA_DENOM_ROWS, MOBA_BLOCK), BF16)
        for h in range(HEADS_PER_LANE_TILE):
            cols = slice(h * tq, (h + 1) * tq)
            v_h = jnp.concatenate([vt[h * HEAD_DIM:(h + 1) * HEAD_DIM], ones], axis=0)
            acc_ref[h] = alpha_ref[:, cols] * acc_ref[h] + jnp.dot(v_h, p_ref[:, cols], preferred_element_type=F32)

    s_refs, p_refs, alpha_refs = (s0_ref, s1_ref), (p0_ref, p1_ref), (a0_ref, a1_ref)

    scores(0, s_refs[0])
    key = lax.broadcasted_iota(jnp.int32, (MOBA_BLOCK, LANES), 0)
    qry = lax.broadcasted_iota(jnp.int32, (MOBA_BLOCK, LANES), 1)
    for c in range(n_cols // LANES):
        cols = slice(c * LANES, (c + 1) * LANES)
        s = jnp.where(key <= qry + (c * LANES) % tq, s_refs[0][:, cols], NEG_INF)
        m0 = jnp.max(s, axis=0, keepdims=True)
        p_refs[0][:, cols] = jnp.exp2(s - m0).astype(BF16)
        m_ref[:, cols] = m0
    alpha_refs[0][...] = jnp.ones_like(alpha_refs[0])
    acc_ref[...] = jnp.zeros_like(acc_ref)
    scores(1, s_refs[1])

    def tick_pair(d, carry):
        for half in range(2):
            t = 2 * d + 1 + half
            cur, prev = (1 - half), half
            scores(t + 1, s_refs[prev])
            softmax_past(t, s_refs[cur], p_refs[cur], alpha_refs[cur])
            accumulate(t - 1, p_refs[prev], alpha_refs[prev])
        return carry

    lax.fori_loop(0, i // 2, tick_pair, 0)

    @pl.when(i % 2 == 1)
    def _():
        accumulate(i - 1, p_refs[0], alpha_refs[0])
        softmax_past(i, s_refs[1], p_refs[1], alpha_refs[1])
        accumulate(i, p_refs[1], alpha_refs[1])

    @pl.when(i % 2 == 0)
    def _():
        accumulate(i, p_refs[0], alpha_refs[0])

    outs = [acc_ref[h, :HEAD_DIM, :] / acc_ref[h, HEAD_DIM:HEAD_DIM + 1, :] for h in range(HEADS_PER_LANE_TILE)]
    return jnp.concatenate(outs, axis=0).T.astype(BF16)


def _moba(q, k, vt, kmean):
    B, S, _ = q.shape
    tq = MOBA_BLOCK
    nb = S // MOBA_BLOCK
    n_pairs = ATT_HEADS // HEADS_PER_LANE_TILE
    n_cols = HEADS_PER_LANE_TILE * tq
    seq = pl.BlockSpec((1, S, LANES), lambda b, hp: (b, 0, hp))
    return pl.pallas_call(
        _moba_kernel,
        grid=(B, n_pairs),
        in_specs=[seq, seq,
                  pl.BlockSpec((1, LANES, S), lambda b, hp: (b, hp, 0)),
                  pl.BlockSpec((1, nb, LANES), lambda b, hp: (b, 0, hp))],
        out_specs=seq,
        out_shape=jax.ShapeDtypeStruct((B, S, ATT_WIDTH), BF16),
        scratch_shapes=[pltpu.VMEM((1, n_cols), F32),
                        pltpu.VMEM((HEADS_PER_LANE_TILE, HEAD_DIM + MOBA_DENOM_ROWS, tq), F32),
                        pltpu.VMEM((1, n_cols), jnp.int32),
                        pltpu.VMEM((MOBA_BLOCK, n_cols), F32),
                        pltpu.VMEM((MOBA_BLOCK, n_cols), F32),
                        pltpu.VMEM((MOBA_BLOCK, n_cols), BF16),
                        pltpu.VMEM((MOBA_BLOCK, n_cols), BF16),
                        pltpu.VMEM((1, n_cols), F32),
                        pltpu.VMEM((1, n_cols), F32)],
        compiler_params=pltpu.CompilerParams(dimension_semantics=("arbitrary", "arbitrary"),
                                             vmem_limit_bytes=VMEM_LIMIT_BYTES),
        name="moba",
    )(q, k, vt, kmean)


ROUTER_GROUP_LANE0 = N_EXPERTS
ROUTE_ROWS = 8


def _mixer_out_kernel(att_ref, a_ref, gb_ref, x_ref, mod_ref, wbb_ref, wout_ref, g2_ref, wr_ref, br_ref,
                      x1_ref, h2_ref, ri_ref, rw_ref, cnt_ref, run_ref):
    tm = x_ref.shape[1]

    @pl.when((pl.program_id(0) == 0) & (pl.program_id(1) == 0))
    def _():
        run_ref[...] = jnp.zeros_like(run_ref)

    yb = jnp.dot(att_ref[0], wbb_ref[...], preferred_element_type=F32)
    merged = a_ref[0].astype(F32) + gb_ref[0].astype(F32) * yb
    mo = jnp.dot(merged.astype(BF16), wout_ref[...], preferred_element_type=F32)
    g1 = mod_ref[0, 2:3, :]
    sh2 = mod_ref[0, 3:4, :]
    sc2 = mod_ref[0, 4:5, :]
    x1 = x_ref[0] + g1 * mo
    x1_ref[0] = x1
    h2 = (_rms(x1) * g2_ref[...]) * (1.0 + sc2) + sh2
    h2_ref[0] = h2

    h_hi = h2.astype(BF16)
    h_lo = (h2 - h_hi.astype(F32)).astype(BF16)
    parts = jnp.dot(jnp.concatenate([h_hi, h_lo], axis=0), wr_ref[...], preferred_element_type=F32)
    logits = (parts[:tm, :LANES] + parts[tm:, :LANES] + parts[:tm, LANES:]) + br_ref[...]
    lane = lax.broadcasted_iota(jnp.int32, (tm, LANES), 1)
    is_group = (lane >= ROUTER_GROUP_LANE0) & (lane < ROUTER_GROUP_LANE0 + N_GROUPS)
    gl = jnp.where(is_group, logits, NEG_INF)
    ge = jnp.exp(gl - jnp.max(gl, axis=1, keepdims=True))
    g_prob = jnp.where(is_group, ge / jnp.sum(ge, axis=1, keepdims=True), -1.0)
    g_p = jnp.max(g_prob, axis=1, keepdims=True)
    lane_f = lane.astype(F32)
    g_idx = _first_index_of(g_prob == g_p, lane_f, float(LANES)) - float(ROUTER_GROUP_LANE0)
    in_group = (lane_f >= g_idx * EXPERTS_PER_GROUP) & (lane_f < (g_idx + 1.0) * EXPERTS_PER_GROUP)
    el = jnp.where(in_group, logits, NEG_INF)
    ee = jnp.exp(el - jnp.max(el, axis=1, keepdims=True))
    e_prob = jnp.where(in_group, ee / jnp.sum(ee, axis=1, keepdims=True), -1.0)
    p1 = jnp.max(e_prob, axis=1, keepdims=True)
    i1 = _first_index_of(e_prob == p1, lane_f, float(LANES))
    e_prob2 = jnp.where(lane_f == i1, -1.0, e_prob)
    p2 = jnp.max(e_prob2, axis=1, keepdims=True)
    i2 = _first_index_of(e_prob2 == p2, lane_f, float(LANES))
    w1 = g_p * p1 / (p1 + p2)
    w2 = g_p * p2 / (p1 + p2)

    hit1 = lane_f == i1
    hit2 = lane_f == i2
    onehot = jnp.where(hit1 | hit2, 1.0, 0.0)
    r_i = lax.broadcasted_iota(jnp.int32, (tm, tm), 0)
    c_i = lax.broadcasted_iota(jnp.int32, (tm, tm), 1)
    earlier = jnp.where(r_i > c_i, 1.0, 0.0).astype(BF16)
    cum = jnp.dot(earlier, onehot.astype(BF16), preferred_element_type=F32) + run_ref[...]
    rank1 = jnp.sum(jnp.where(hit1, cum, 0.0), axis=1, keepdims=True).astype(jnp.int32)
    rank2 = jnp.sum(jnp.where(hit2, cum, 0.0), axis=1, keepdims=True).astype(jnp.int32)
    run_ref[...] = run_ref[...] + jnp.sum(onehot, axis=0, keepdims=True)
    cnt_ref[...] = run_ref[...]

    ri = jnp.where(lane == 0, i1.astype(jnp.int32), jnp.where(lane == 1, i2.astype(jnp.int32),
                   jnp.where(lane == 2, rank1, jnp.where(lane == 3, rank2, 0))))
    ri_ref[...] = ri.T[:ROUTE_ROWS, :]
    rw_ref[0] = jnp.where(lane == 0, w1, jnp.where(lane == 1, w2, 0.0))


def _mixer_out(att, a_out, gb, x, mod3, w_branch_b, w_out, g2, w_router, b_router):
    B, S, D = x.shape
    tm = TOKEN_TILE
    tok = lambda w: pl.BlockSpec((1, tm, w), lambda b, i: (b, i, 0))
    return pl.pallas_call(
        _mixer_out_kernel,
        grid=(B, S // tm),
        in_specs=[tok(ATT_WIDTH), tok(D), tok(D), tok(D),
                  pl.BlockSpec((1, N_MOD, D), lambda b, i: (b, 0, 0)),
                  _const_spec(w_branch_b.shape),
                  _const_spec(w_out.shape),
                  _const_spec((1, D)),
                  _const_spec(w_router.shape),
                  _const_spec((1, LANES))],
        out_specs=[tok(D), tok(D),
                   pl.BlockSpec((ROUTE_ROWS, tm), lambda b, i: (0, b * (S // tm) + i)), tok(LANES),
                   pl.BlockSpec((1, LANES), lambda b, i: (0, 0))],
        out_shape=[jax.ShapeDtypeStruct((B, S, D), F32),
                   jax.ShapeDtypeStruct((B, S, D), F32),
                   jax.ShapeDtypeStruct((ROUTE_ROWS, B * S), jnp.int32),
                   jax.ShapeDtypeStruct((B, S, LANES), F32),
                   jax.ShapeDtypeStruct((1, LANES), F32)],
        scratch_shapes=[pltpu.VMEM((1, LANES), F32)],
        compiler_params=pltpu.CompilerParams(dimension_semantics=("arbitrary", "arbitrary"),
                                             vmem_limit_bytes=VMEM_LIMIT_BYTES),
        name="mixer_out",
    )(att, a_out, gb, x, mod3, w_branch_b, w_out, g2, w_router, b_router)


def _row_copies(src_ref, dst_ref, src_rows, dst_rows, r, sem):
    return pltpu.make_async_copy(src_ref.at[pl.ds(src_rows(r), 1), :], dst_ref.at[pl.ds(dst_rows(r), 1), :], sem)


def _dispatch_kernel(tail_ref, d1_ref, d2_ref, h_ref, xb_ref, zero_ref, sem, zero_sem):
    n = h_ref.shape[0]

    @pl.when(pl.program_id(0) == 0)
    def _():
        zero_ref[...] = jnp.zeros_like(zero_ref)
        zero_block = lambda e: pltpu.make_async_copy(
            zero_ref, xb_ref.at[pl.ds(pl.multiple_of(tail_ref[e], DISPATCH_BLOCK), DISPATCH_BLOCK), :], zero_sem)
        for phase in ("start", "wait"):
            for e in range(tail_ref.shape[0]):
                @pl.when(tail_ref[e] >= 0)
                def _(e=e, phase=phase):
                    getattr(zero_block(e), phase)()

    same = lambda r: r
    copies = [functools.partial(_row_copies, h_ref, xb_ref, same, lambda r, d=d: d[0, 0, r])
              for d in (d1_ref, d2_ref)]

    def start(r, carry):
        for cp in copies:
            cp(r, sem).start()
        return carry

    lax.fori_loop(0, n, start, 0, unroll=ROW_DMA_UNROLL)
    for _ in copies:
        pltpu.make_async_copy(h_ref, xb_ref.at[pl.ds(0, n), :], sem).wait()


def _dispatch(h2, dest1, dest2, tail_block_start, n_rows):
    T, D = h2.shape
    n = ROW_DMA_TILE
    idx = pl.BlockSpec((1, 1, n), lambda i, tail: (i, 0, 0), memory_space=pltpu.SMEM)
    grid_spec = pltpu.PrefetchScalarGridSpec(
        num_scalar_prefetch=1,
        grid=(T // n,),
        in_specs=[idx, idx, pl.BlockSpec((n, D), lambda i, tail: (i, 0))],
        out_specs=pl.BlockSpec(memory_space=pl.ANY),
        scratch_shapes=[pltpu.VMEM((DISPATCH_BLOCK, D), F32), pltpu.SemaphoreType.DMA(()),
                        pltpu.SemaphoreType.DMA(())],
    )
    return pl.pallas_call(
        _dispatch_kernel,
        grid_spec=grid_spec,
        out_shape=jax.ShapeDtypeStruct((n_rows, D), F32),
        compiler_params=pltpu.CompilerParams(dimension_semantics=("arbitrary",)),
        name="dispatch",
    )(tail_block_start, dest1.reshape(T // n, 1, n), dest2.reshape(T // n, 1, n), h2)


def _expert_kernel(be_ref, nu_ref, x_ref, wg_ref, wu_ref, wd_ref, y_ref, wgu_bf, wd_bf):
    i = pl.program_id(0)
    used = i < nu_ref[0]

    @pl.when(used & ((i == 0) | (be_ref[i] != be_ref[jnp.maximum(i - 1, 0)])))
    def _():
        wgu_bf[:, :D_EXPERT] = wg_ref[0].astype(BF16)
        wgu_bf[:, D_EXPERT:] = wu_ref[0].astype(BF16)
        wd_bf[...] = wd_ref[0].astype(BF16)

    @pl.when(used)
    def _():
        gate_up = jnp.dot(x_ref[...].astype(BF16), wgu_bf[...], preferred_element_type=F32)
        hid = (jax.nn.silu(gate_up[:, :D_EXPERT]) * gate_up[:, D_EXPERT:]).astype(BF16)
        y_ref[...] = jnp.dot(hid, wd_bf[...], preferred_element_type=F32)

    @pl.when(i >= nu_ref[0])
    def _():
        y_ref[...] = jnp.zeros_like(y_ref)


def _experts(x_buf, block_expert, n_used, w_gate, w_up, w_down):
    P, D = x_buf.shape
    n_blocks = P // DISPATCH_BLOCK
    grid_spec = pltpu.PrefetchScalarGridSpec(
        num_scalar_prefetch=2,
        grid=(n_blocks,),
        in_specs=[pl.BlockSpec((DISPATCH_BLOCK, D), lambda i, be, nu: (jnp.minimum(i, nu[0] - 1), 0)),
                  pl.BlockSpec((1, D, D_EXPERT), lambda i, be, nu: (be[i], 0, 0)),
                  pl.BlockSpec((1, D, D_EXPERT), lambda i, be, nu: (be[i], 0, 0)),
                  pl.BlockSpec((1, D_EXPERT, D), lambda i, be, nu: (be[i], 0, 0))],
        out_specs=pl.BlockSpec((DISPATCH_BLOCK, D), lambda i, be, nu: (i, 0)),
        scratch_shapes=[pltpu.VMEM((D, 2 * D_EXPERT), BF16), pltpu.VMEM((D_EXPERT, D), BF16)],
    )
    return pl.pallas_call(
        _expert_kernel,
        grid_spec=grid_spec,
        out_shape=jax.ShapeDtypeStruct((P, D), F32),
        compiler_params=pltpu.CompilerParams(dimension_semantics=("arbitrary",),
                                             vmem_limit_bytes=VMEM_LIMIT_BYTES),
        name="experts",
    )(block_expert, n_used, x_buf, w_gate, w_up, w_down)


def _combine_kernel(d1_ref, d2_ref, d1n_ref, d2n_ref, x1_ref, rw_ref, mod_ref, fg_ref, yb_ref, o_ref, y_ref, sem):
    n = x1_ref.shape[1]
    step = pl.program_id(0) * pl.num_programs(1) + pl.program_id(1)
    last = pl.num_programs(0) * pl.num_programs(1) - 1
    slot = step % 2

    def gather(dests, buf):
        def start(r, carry):
            for k, d in enumerate(dests):
                pltpu.make_async_copy(yb_ref.at[pl.ds(d[0, 0, r], 1), :], y_ref.at[buf, k, pl.ds(r, 1), :],
                                      sem.at[buf]).start()
            return carry
        lax.fori_loop(0, n, start, 0, unroll=ROW_DMA_UNROLL)

    @pl.when(step == 0)
    def _():
        gather((d1_ref, d2_ref), 0)

    @pl.when(step < last)
    def _():
        gather((d1n_ref, d2n_ref), 1 - slot)

    for k in range(TOP_K_EXPERT):
        pltpu.make_async_copy(yb_ref.at[pl.ds(0, n), :], y_ref.at[slot, k], sem.at[slot]).wait()

    rw = rw_ref[0]
    moe = rw[:, 0:1] * y_ref[slot, 0] + rw[:, 1:2] * y_ref[slot, 1]
    g2 = mod_ref[0, 5:6, :]
    o_ref[0] = _rms(x1_ref[0] + g2 * moe) * fg_ref[...]


def _combine(x1, route_w, mod3, final_g, y_buf, dest1, dest2):
    B, S, D = x1.shape
    n = ROW_DMA_TILE
    nt = S // n
    idx = pl.BlockSpec((1, 1, n), lambda b, i: (b * nt + i, 0, 0), memory_space=pltpu.SMEM)
    idx_next = pl.BlockSpec((1, 1, n), lambda b, i: (jnp.minimum(b * nt + i + 1, B * nt - 1), 0, 0),
                            memory_space=pltpu.SMEM)
    tok = lambda w: pl.BlockSpec((1, n, w), lambda b, i: (b, i, 0))
    d1 = dest1.reshape(B * nt, 1, n)
    d2 = dest2.reshape(B * nt, 1, n)
    return pl.pallas_call(
        _combine_kernel,
        grid=(B, nt),
        in_specs=[idx, idx, idx_next, idx_next, tok(D), tok(LANES),
                  pl.BlockSpec((1, N_MOD, D), lambda b, i: (b, 0, 0)),
                  pl.BlockSpec((1, D), lambda b, i: (0, 0)),
                  pl.BlockSpec(memory_space=pl.ANY)],
        out_specs=tok(D),
        out_shape=jax.ShapeDtypeStruct((B, S, D), F32),
        scratch_shapes=[pltpu.VMEM((2, TOP_K_EXPERT, n, D), F32), pltpu.SemaphoreType.DMA((2,))],
        compiler_params=pltpu.CompilerParams(dimension_semantics=("arbitrary", "arbitrary")),
        name="combine",
    )(d1, d2, d1, d2, x1, route_w, mod3, final_g, y_buf)


def _rope_tables(S):
    half = ROPE_DIM // 2
    inv_freq = jnp.power(ROPE_THETA, -jnp.arange(half, dtype=jnp.float32) * 2.0 / ROPE_DIM)
    ang = jnp.arange(S).astype(jnp.float32)[:, None] * inv_freq[None, :]
    cos = jnp.cos(ang)
    sin = jnp.sin(ang)
    rest = HEAD_DIM - ROPE_DIM
    ones = jnp.ones((S, rest), F32)
    zeros = jnp.zeros((S, rest), F32)
    zh = jnp.zeros((S, half), F32)
    per_head = lambda parts: jnp.tile(jnp.concatenate(parts, axis=1), (1, HEADS_PER_LANE_TILE))
    return per_head([cos, cos, ones]), per_head([-sin, zh, zeros]), per_head([zh, sin, zeros])


def kernel(x, c, ada_w, ada_b, norm1_g, norm2_g, w_in, gmlp_ln_g, gmlp_ln_b, w_spatial, b_spatial, w_branch_a,
           w_branch_b, w_out, w_router_group, b_router_group, w_router_expert, b_router_expert, w_gate, w_up,
           w_down, final_norm_g):
    B, S, D = x.shape
    T = B * S
    assert D == D_MODEL and S % TOKEN_TILE == 0 and S % MOBA_BLOCK == 0 and S // MOBA_BLOCK <= 32
    assert ada_w.shape[0] == 1, "single layer"
    cos_t, sin_a, sin_b = _rope_tables(S)

    mod3 = _ada_mod(c, ada_w[0], ada_b).reshape(B, N_MOD, D)

    bias_sp = jnp.repeat(b_spatial[0].T, GMLP_CH, axis=1)
    q, k, v, a_out, gb, kmean = _mixer_in(
        x, mod3, norm1_g, w_in[0].astype(BF16), w_spatial[0], bias_sp, gmlp_ln_g, gmlp_ln_b,
        w_branch_a[0].astype(BF16), cos_t, sin_a, sin_b)
    att = _moba(q, k, v, kmean.reshape(B, S // MOBA_BLOCK, ATT_WIDTH))

    w_router = jnp.zeros((D, LANES), F32)
    w_router = w_router.at[:, :N_EXPERTS].set(w_router_expert[0].reshape(D, N_EXPERTS))
    w_router = w_router.at[:, ROUTER_GROUP_LANE0:ROUTER_GROUP_LANE0 + N_GROUPS].set(w_router_group[0])
    w_router_hi = w_router.astype(BF16)
    w_router = jnp.concatenate([w_router_hi, (w_router - w_router_hi.astype(F32)).astype(BF16)], axis=1)
    b_router = jnp.zeros((1, LANES), F32)
    b_router = b_router.at[0, :N_EXPERTS].set(b_router_expert[0].reshape(N_EXPERTS))
    b_router = b_router.at[0, ROUTER_GROUP_LANE0:ROUTER_GROUP_LANE0 + N_GROUPS].set(b_router_group[0])
    x1, h2, route_i, route_w, counts = _mixer_out(
        att, a_out, gb, x, mod3, w_branch_b[0].astype(BF16), w_out[0].astype(BF16), norm2_g, w_router, b_router)

    counts = counts[0, :N_EXPERTS].astype(jnp.int32)
    padded = (counts + DISPATCH_BLOCK - 1) // DISPATCH_BLOCK * DISPATCH_BLOCK
    pend = jnp.cumsum(padded)
    pstart = pend - padded
    n_blocks = -(-(T * TOP_K_EXPERT) // DISPATCH_BLOCK) + N_EXPERTS
    block_start = jnp.arange(n_blocks, dtype=jnp.int32) * DISPATCH_BLOCK
    block_expert = jnp.minimum(
        jnp.sum((pend[None, :] <= block_start[:, None]).astype(jnp.int32), axis=1), N_EXPERTS - 1)
    n_used = (pend[-1:] // DISPATCH_BLOCK).astype(jnp.int32)
    expert_ids = jnp.arange(N_EXPERTS, dtype=jnp.int32)

    def slot(expert, rank):
        return jnp.sum(jnp.where(expert[:, None] == expert_ids[None, :], pstart[None, :], 0), axis=1) + rank

    dest1 = slot(route_i[0], route_i[2])
    dest2 = slot(route_i[1], route_i[3])

    spare = (n_used[0] + jnp.arange(N_EXPERTS, dtype=jnp.int32)) * DISPATCH_BLOCK
    tail_block_start = jnp.concatenate([
        jnp.where(counts > 0, pend - DISPATCH_BLOCK, -1),
        jnp.where(spare < n_blocks * DISPATCH_BLOCK, spare, -1)]).astype(jnp.int32)
    x_buf = _dispatch(h2.reshape(T, D), dest1, dest2, tail_block_start, n_blocks * DISPATCH_BLOCK)
    y_buf = _experts(x_buf, block_expert, n_used, w_gate[0], w_up[0], w_down[0])
    return _combine(x1, route_w, mod3, final_norm_g.reshape(1, D), y_buf, dest1, dest2)
```

```python
import functools

import jax
import jax.numpy as jnp
from jax import lax
from jax.experimental import pallas as pl
from jax.experimental.pallas import tpu as pltpu

D_MODEL = 1024
GMLP_WIDTH = D_MODEL // 2
GMLP_GROUPS = 4
GMLP_CH = GMLP_WIDTH // GMLP_GROUPS
GMLP_CHUNK = 128
ATT_HEADS = 8
HEAD_DIM = 64
ATT_WIDTH = ATT_HEADS * HEAD_DIM
ROPE_DIM = HEAD_DIM // 4
ROPE_THETA = 500000.0
MOBA_BLOCK = 256
MOBA_TOPK = 3
N_GROUPS = 4
EXPERTS_PER_GROUP = 8
N_EXPERTS = N_GROUPS * EXPERTS_PER_GROUP
TOP_K_EXPERT = 2
D_EXPERT = D_MODEL // 2
DISPATCH_BLOCK = 256
N_MOD = 6
EPS = 1e-6

LANES = 128
HEADS_PER_LANE_TILE = LANES // HEAD_DIM
VMEM_LIMIT_BYTES = 56 * 1024 * 1024

TOKEN_TILE = 512
ROW_DMA_TILE = 256
ROW_DMA_UNROLL = 8

F32 = jnp.float32
BF16 = jnp.bfloat16
NEG_INF = float("-inf")
LOG2_E = 1.4426950408889634


def _rms(x):
    return x * lax.rsqrt(jnp.mean(x * x, axis=-1, keepdims=True) + EPS)


def _first_index_of(mask, idx, sentinel):
    return jnp.min(jnp.where(mask, idx, sentinel), axis=1, keepdims=True)


def _ada_kernel(c_ref, w_ref, b_ref, o_ref):
    o_ref[...] = jnp.dot(c_ref[...], w_ref[...], preferred_element_type=F32,
                         precision=lax.Precision.HIGHEST) + b_ref[...]


def _ada_mod(c, ada_w, ada_b):
    B = c.shape[0]
    return pl.pallas_call(
        _ada_kernel,
        grid=(N_MOD,),
        in_specs=[pl.BlockSpec((B, D_MODEL), lambda j: (0, 0)),
                  pl.BlockSpec((D_MODEL, D_MODEL), lambda j: (0, j)),
                  pl.BlockSpec((1, D_MODEL), lambda j: (0, j))],
        out_specs=pl.BlockSpec((B, D_MODEL), lambda j: (0, j)),
        out_shape=jax.ShapeDtypeStruct((B, N_MOD * D_MODEL), F32),
        name="ada_mod",
    )(c, ada_w, ada_b)


def _mixer_in_kernel(x_ref, mod_ref, g1_ref, w_in_ref, wsp_ref, bsp_ref, lng_ref, lnb_ref, wba_ref,
                     cos_ref, sa_ref, sb_ref,
                     q_ref, k_ref, v_ref, a_ref, gb_ref, km_ref):
    tm = x_ref.shape[1]
    sh1 = mod_ref[0, 0:1, :]
    sc1 = mod_ref[0, 1:2, :]
    h = (_rms(x_ref[0]) * g1_ref[...]) * (1.0 + sc1) + sh1
    hb = h.astype(BF16)

    off = 2 * GMLP_WIDTH + 3 * ATT_WIDTH
    pa = jnp.dot(hb, w_in_ref[:, 0:2 * GMLP_WIDTH], preferred_element_type=F32)
    gbv = jnp.dot(hb, w_in_ref[:, off + D_MODEL:off + 2 * D_MODEL], preferred_element_type=F32)
    qkv = jnp.dot(hb, w_in_ref[:, 2 * GMLP_WIDTH:2 * GMLP_WIDTH + 3 * ATT_WIDTH], preferred_element_type=F32)
    ga = jnp.dot(hb, w_in_ref[:, off:off + D_MODEL], preferred_element_type=F32)
    gb_ref[0] = jax.nn.sigmoid(gbv).astype(BF16)

    z = jax.nn.gelu(pa)
    u = z[:, :GMLP_WIDTH]
    v = z[:, GMLP_WIDTH:]
    mu = jnp.mean(v, axis=-1, keepdims=True)
    var = jnp.mean(jnp.square(v - mu), axis=-1, keepdims=True)
    vn = ((v - mu) * lax.rsqrt(var + EPS) * lng_ref[...] + lnb_ref[...]).astype(BF16)
    t_idx = lax.broadcasted_iota(jnp.int32, (GMLP_CHUNK, GMLP_CHUNK), 0)
    s_idx = lax.broadcasted_iota(jnp.int32, (GMLP_CHUNK, GMLP_CHUNK), 1)
    w_causal = [jnp.where(t_idx >= s_idx, wsp_ref[g], 0.0).astype(BF16) for g in range(GMLP_GROUPS)]
    chunks = []
    for c in range(tm // GMLP_CHUNK):
        rows = slice(c * GMLP_CHUNK, (c + 1) * GMLP_CHUNK)
        cols = [jnp.dot(w_causal[g], vn[rows, g * GMLP_CH:(g + 1) * GMLP_CH], preferred_element_type=F32)
                for g in range(GMLP_GROUPS)]
        chunks.append(jnp.concatenate(cols, axis=1) + bsp_ref[...])
    sv = jnp.concatenate(chunks, axis=0)
    ya = jnp.dot((u * sv).astype(BF16), wba_ref[...], preferred_element_type=F32)

    a_ref[0] = (jax.nn.sigmoid(ga) * ya).astype(BF16)

    cos_t = cos_ref[...]
    sin_a = sa_ref[...]
    sin_b = sb_ref[...]
    half = ROPE_DIM // 2

    def rope(t):
        outs = []
        for j in range(ATT_WIDTH // LANES):
            tj = t[:, j * LANES:(j + 1) * LANES]
            outs.append(tj * cos_t + pltpu.roll(tj, LANES - half, 1) * sin_a + pltpu.roll(tj, half, 1) * sin_b)
        return jnp.concatenate(outs, axis=1)

    q = rope(qkv[:, :ATT_WIDTH]) * (HEAD_DIM ** -0.5 * LOG2_E)
    k = rope(qkv[:, ATT_WIDTH:2 * ATT_WIDTH])
    q_ref[0] = q.astype(BF16)
    k_ref[0] = k.astype(BF16)
    v_ref[0] = qkv[:, 2 * ATT_WIDTH:].T.astype(BF16)
    for r in range(tm // MOBA_BLOCK):
        km_ref[0, 0, r:r + 1, :] = jnp.mean(k[r * MOBA_BLOCK:(r + 1) * MOBA_BLOCK], axis=0, keepdims=True)


def _const_spec(shape):
    nd = len(shape)
    return pl.BlockSpec(shape, lambda *_: (0,) * nd, pipeline_mode=pl.Buffered(1))


def _mixer_in(x, mod3, g1, w_in, w_spatial, bias_sp, ln_g, ln_b, w_branch_a, cos_t, sin_a, sin_b):
    B, S, D = x.shape
    tm = TOKEN_TILE
    nt = S // tm
    tok = lambda w: pl.BlockSpec((1, tm, w), lambda b, i: (b, i, 0))
    tab = pl.BlockSpec((tm, LANES), lambda b, i: (i, 0))
    return pl.pallas_call(
        _mixer_in_kernel,
        grid=(B, nt),
        in_specs=[tok(D),
                  pl.BlockSpec((1, N_MOD, D), lambda b, i: (b, 0, 0)),
                  _const_spec((1, D)),
                  _const_spec(w_in.shape),
                  _const_spec(w_spatial.shape),
                  _const_spec(bias_sp.shape),
                  _const_spec((1, GMLP_WIDTH)),
                  _const_spec((1, GMLP_WIDTH)),
                  _const_spec(w_branch_a.shape),
                  tab, tab, tab],
        out_specs=[tok(ATT_WIDTH), tok(ATT_WIDTH),
                   pl.BlockSpec((1, ATT_WIDTH, tm), lambda b, i: (b, 0, i)), tok(D), tok(D),
                   pl.BlockSpec((1, 1, tm // MOBA_BLOCK, ATT_WIDTH), lambda b, i: (b, i, 0, 0))],
        out_shape=[jax.ShapeDtypeStruct((B, S, ATT_WIDTH), BF16)] * 2
                  + [jax.ShapeDtypeStruct((B, ATT_WIDTH, S), BF16)]
                  + [jax.ShapeDtypeStruct((B, S, D), BF16)] * 2
                  + [jax.ShapeDtypeStruct((B, nt, tm // MOBA_BLOCK, ATT_WIDTH), F32)],
        compiler_params=pltpu.CompilerParams(dimension_semantics=("arbitrary", "arbitrary"),
                                             vmem_limit_bytes=VMEM_LIMIT_BYTES),
        name="mixer_in",
    )(x, mod3, g1, w_in, w_spatial, bias_sp, ln_g, ln_b, w_branch_a, cos_t, sin_a, sin_b)


MOBA_DENOM_ROWS = 16


def _moba_kernel(q_ref, k_ref, vt_ref, km_ref, o_ref, *scratch):
    tq = MOBA_BLOCK

    def tile(i, carry):
        rows = pl.ds(pl.multiple_of(i * tq, tq), tq)
        o_ref[0, rows, :] = _moba_tile(i, q_ref[0, rows, :], k_ref, vt_ref, km_ref, *scratch)
        return carry

    lax.fori_loop(0, q_ref.shape[1] // tq, tile, 0)


def _moba_tile(i, q2, k_ref, vt_ref, km_ref, m_ref, acc_ref, bits_ref,
               s0_ref, s1_ref, p0_ref, p1_ref, a0_ref, a1_ref):
    tq = q2.shape[0]
    nb = km_ref.shape[1]
    n_cols = HEADS_PER_LANE_TILE * tq
    q_t = q2.astype(F32).T
    feat = lax.broadcasted_iota(jnp.int32, (LANES, tq), 0)
    qs_t = jnp.concatenate([jnp.where(feat < HEAD_DIM, q_t, 0.0), jnp.where(feat >= HEAD_DIM, q_t, 0.0)],
                           axis=1).astype(BF16)

    km = km_ref[0]
    km_hi = km.astype(BF16)
    km_lo = (km - km_hi.astype(F32)).astype(BF16)
    gate = (jnp.dot(km_hi, qs_t, preferred_element_type=F32)
            + jnp.dot(km_lo, qs_t, preferred_element_type=F32))
    n_idx = lax.broadcasted_iota(jnp.int32, (nb, n_cols), 0)
    g = jnp.where(n_idx < i, gate, NEG_INF)
    bits = jnp.zeros((1, n_cols), jnp.int32)
    for r in range(MOBA_TOPK):
        mx = jnp.max(g, axis=0, keepdims=True)
        first = jnp.min(jnp.where(g == mx, n_idx, nb), axis=0, keepdims=True)
        bits = bits | jnp.where(r < i, jnp.left_shift(1, first), 0)
        g = jnp.where(n_idx == first, NEG_INF, g)
    bits_ref[...] = bits

    def block_start(t):
        return pl.multiple_of(jnp.where(t == 0, i, t - 1) * MOBA_BLOCK, MOBA_BLOCK)

    def scores(t, s_ref):
        t = jnp.minimum(t, i)
        s_ref[...] = jnp.dot(k_ref[0, pl.ds(block_start(t), MOBA_BLOCK), :], qs_t, preferred_element_type=F32)

    def softmax_past(t, s_ref, p_ref, alpha_ref):
        j = jnp.minimum(t - 1, nb - 1)
        for c in range(n_cols // LANES):
            cols = slice(c * LANES, (c + 1) * LANES)
            s = s_ref[:, cols]
            picked = (lax.shift_right_logical(bits_ref[:, cols], j) & 1) == 1
            m_old = m_ref[:, cols]
            m_new = jnp.where(picked, jnp.maximum(m_old, jnp.max(s, axis=0, keepdims=True)), m_old)
            alpha_ref[:, cols] = jnp.exp2(m_old - m_new)
            m_sub = jnp.where(picked, m_new, jnp.inf)
            p_ref[:, cols] = jnp.exp2(s - m_sub).astype(BF16)
            m_ref[:, cols] = m_new

    def accumulate(t, p_ref, alpha_ref):
        vt = vt_ref[0, :, pl.ds(block_start(t), MOBA_BLOCK)]
        ones = jnp.ones((MOBA_DENOM_ROWS, MOBA_BLOCK), BF16)
        for h in range(HEADS_PER_LANE_TILE):
            cols = slice(h * tq, (h + 1) * tq)
            v_h = jnp.concatenate([vt[h * HEAD_DIM:(h + 1) * HEAD_DIM], ones], axis=0)
            acc_ref[h] = alpha_ref[:, cols] * acc_ref[h] + jnp.dot(v_h, p_ref[:, cols], preferred_element_type=F32)

    s_refs, p_refs, alpha_refs = (s0_ref, s1_ref), (p0_ref, p1_ref), (a0_ref, a1_ref)

    scores(0, s_refs[0])
    key = lax.broadcasted_iota(jnp.int32, (MOBA_BLOCK, LANES), 0)
    qry = lax.broadcasted_iota(jnp.int32, (MOBA_BLOCK, LANES), 1)
    for c in range(n_cols // LANES):
        cols = slice(c * LANES, (c + 1) * LANES)
        s = jnp.where(key <= qry + (c * LANES) % tq, s_refs[0][:, cols], NEG_INF)
        m0 = jnp.max(s, axis=0, keepdims=True)
        p_refs[0][:, cols] = jnp.exp2(s - m0).astype(BF16)
        m_ref[:, cols] = m0
    alpha_refs[0][...] = jnp.ones_like(alpha_refs[0])
    acc_ref[...] = jnp.zeros_like(acc_ref)
    scores(1, s_refs[1])

    def tick_pair(d, carry):
        for half in range(2):
            t = 2 * d + 1 + half
            cur, prev = (1 - half), half
            scores(t + 1, s_refs[prev])
            softmax_past(t, s_refs[cur], p_refs[cur], alpha_refs[cur])
            accumulate(t - 1, p_refs[prev], alpha_refs[prev])
        return carry

    lax.fori_loop(0, i // 2, tick_pair, 0)

    @pl.when(i % 2 == 1)
    def _():
        accumulate(i - 1, p_refs[0], alpha_refs[0])
        softmax_past(i, s_refs[1], p_refs[1], alpha_refs[1])
        accumulate(i, p_refs[1], alpha_refs[1])

    @pl.when(i % 2 == 0)
    def _():
        accumulate(i, p_refs[0], alpha_refs[0])

    outs = [acc_ref[h, :HEAD_DIM, :] / acc_ref[h, HEAD_DIM:HEAD_DIM + 1, :] for h in range(HEADS_PER_LANE_TILE)]
    return jnp.concatenate(outs, axis=0).T.astype(BF16)


def _moba(q, k, vt, kmean):
    B, S, _ = q.shape
    tq = MOBA_BLOCK
    nb = S // MOBA_BLOCK
    n_pairs = ATT_HEADS // HEADS_PER_LANE_TILE
    n_cols = HEADS_PER_LANE_TILE * tq
    seq = pl.BlockSpec((1, S, LANES), lambda b, hp: (b, 0, hp))
    return pl.pallas_call(
        _moba_kernel,
        grid=(B, n_pairs),
        in_specs=[seq, seq,
                  pl.BlockSpec((1, LANES, S), lambda b, hp: (b, hp, 0)),
                  pl.BlockSpec((1, nb, LANES), lambda b, hp: (b, 0, hp))],
        out_specs=seq,
        out_shape=jax.ShapeDtypeStruct((B, S, ATT_WIDTH), BF16),
        scratch_shapes=[pltpu.VMEM((1, n_cols), F32),
                        pltpu.VMEM((HEADS_PER_LANE_TILE, HEAD_DIM + MOBA_DENOM_ROWS, tq), F32),
                        pltpu.VMEM((1, n_cols), jnp.int32),
                        pltpu.VMEM((MOBA_BLOCK, n_cols), F32),
                        pltpu.VMEM((MOBA_BLOCK, n_cols), F32),
                        pltpu.VMEM((MOBA_BLOCK, n_cols), BF16),
                        pltpu.VMEM((MOBA_BLOCK, n_cols), BF16),
                        pltpu.VMEM((1, n_cols), F32),
                        pltpu.VMEM((1, n_cols), F32)],
        compiler_params=pltpu.CompilerParams(dimension_semantics=("arbitrary", "arbitrary"),
                                             vmem_limit_bytes=VMEM_LIMIT_BYTES),
        name="moba",
    )(q, k, vt, kmean)


ROUTER_GROUP_LANE0 = N_EXPERTS
ROUTE_ROWS = 8


def _mixer_out_kernel(att_ref, a_ref, gb_ref, x_ref, mod_ref, wbb_ref, wout_ref, g2_ref, wr_ref, br_ref,
                      x1_ref, h2_ref, ri_ref, rw_ref, cnt_ref, run_ref):
    tm = x_ref.shape[1]

    @pl.when((pl.program_id(0) == 0) & (pl.program_id(1) == 0))
    def _():
        run_ref[...] = jnp.zeros_like(run_ref)

    yb = jnp.dot(att_ref[0], wbb_ref[...], preferred_element_type=F32)
    merged = a_ref[0].astype(F32) + gb_ref[0].astype(F32) * yb
    mo = jnp.dot(merged.astype(BF16), wout_ref[...], preferred_element_type=F32)
    g1 = mod_ref[0, 2:3, :]
    sh2 = mod_ref[0, 3:4, :]
    sc2 = mod_ref[0, 4:5, :]
    x1 = x_ref[0] + g1 * mo
    x1_ref[0] = x1
    h2 = (_rms(x1) * g2_ref[...]) * (1.0 + sc2) + sh2
    h2_ref[0] = h2

    h_hi = h2.astype(BF16)
    h_lo = (h2 - h_hi.astype(F32)).astype(BF16)
    parts = jnp.dot(jnp.concatenate([h_hi, h_lo], axis=0), wr_ref[...], preferred_element_type=F32)
    logits = (parts[:tm, :LANES] + parts[tm:, :LANES] + parts[:tm, LANES:]) + br_ref[...]
    lane = lax.broadcasted_iota(jnp.int32, (tm, LANES), 1)
    is_group = (lane >= ROUTER_GROUP_LANE0) & (lane < ROUTER_GROUP_LANE0 + N_GROUPS)
    gl = jnp.where(is_group, logits, NEG_INF)
    ge = jnp.exp(gl - jnp.max(gl, axis=1, keepdims=True))
    g_prob = jnp.where(is_group, ge / jnp.sum(ge, axis=1, keepdims=True), -1.0)
    g_p = jnp.max(g_prob, axis=1, keepdims=True)
    lane_f = lane.astype(F32)
    g_idx = _first_index_of(g_prob == g_p, lane_f, float(LANES)) - float(ROUTER_GROUP_LANE0)
    in_group = (lane_f >= g_idx * EXPERTS_PER_GROUP) & (lane_f < (g_idx + 1.0) * EXPERTS_PER_GROUP)
    el = jnp.where(in_group, logits, NEG_INF)
    ee = jnp.exp(el - jnp.max(el, axis=1, keepdims=True))
    e_prob = jnp.where(in_group, ee / jnp.sum(ee, axis=1, keepdims=True), -1.0)
    p1 = jnp.max(e_prob, axis=1, keepdims=True)
    i1 = _first_index_of(e_prob == p1, lane_f, float(LANES))
    e_prob2 = jnp.where(lane_f == i1, -1.0, e_prob)
    p2 = jnp.max(e_prob2, axis=1, keepdims=True)
    i2 = _first_index_of(e_prob2 == p2, lane_f, float(LANES))
    w1 = g_p * p1 / (p1 + p2)
    w2 = g_p * p2 / (p1 + p2)

    hit1 = lane_f == i1
    hit2 = lane_f == i2
    onehot = jnp.where(hit1 | hit2, 1.0, 0.0)
    r_i = lax.broadcasted_iota(jnp.int32, (tm, tm), 0)
    c_i = lax.broadcasted_iota(jnp.int32, (tm, tm), 1)
    earlier = jnp.where(r_i > c_i, 1.0, 0.0).astype(BF16)
    cum = jnp.dot(earlier, onehot.astype(BF16), preferred_element_type=F32) + run_ref[...]
    rank1 = jnp.sum(jnp.where(hit1, cum, 0.0), axis=1, keepdims=True).astype(jnp.int32)
    rank2 = jnp.sum(jnp.where(hit2, cum, 0.0), axis=1, keepdims=True).astype(jnp.int32)
    run_ref[...] = run_ref[...] + jnp.sum(onehot, axis=0, keepdims=True)
    cnt_ref[...] = run_ref[...]

    ri = jnp.where(lane == 0, i1.astype(jnp.int32), jnp.where(lane == 1, i2.astype(jnp.int32),
                   jnp.where(lane == 2, rank1, jnp.where(lane == 3, rank2, 0))))
    ri_ref[...] = ri.T[:ROUTE_ROWS, :]
    rw_ref[0] = jnp.where(lane == 0, w1, jnp.where(lane == 1, w2, 0.0))


def _mixer_out(att, a_out, gb, x, mod3, w_branch_b, w_out, g2, w_router, b_router):
    B, S, D = x.shape
    tm = TOKEN_TILE
    tok = lambda w: pl.BlockSpec((1, tm, w), lambda b, i: (b, i, 0))
    return pl.pallas_call(
        _mixer_out_kernel,
        grid=(B, S // tm),
        in_specs=[tok(ATT_WIDTH), tok(D), tok(D), tok(D),
                  pl.BlockSpec((1, N_MOD, D), lambda b, i: (b, 0, 0)),
                  _const_spec(w_branch_b.shape),
                  _const_spec(w_out.shape),
                  _const_spec((1, D)),
                  _const_spec(w_router.shape),
                  _const_spec((1, LANES))],
        out_specs=[tok(D), tok(D),
                   pl.BlockSpec((ROUTE_ROWS, tm), lambda b, i: (0, b * (S // tm) + i)), tok(LANES),
                   pl.BlockSpec((1, LANES), lambda b, i: (0, 0))],
        out_shape=[jax.ShapeDtypeStruct((B, S, D), F32),
                   jax.ShapeDtypeStruct((B, S, D), F32),
                   jax.ShapeDtypeStruct((ROUTE_ROWS, B * S), jnp.int32),
                   jax.ShapeDtypeStruct((B, S, LANES), F32),
                   jax.ShapeDtypeStruct((1, LANES), F32)],
        scratch_shapes=[pltpu.VMEM((1, LANES), F32)],
        compiler_params=pltpu.CompilerParams(dimension_semantics=("arbitrary", "arbitrary"),
                                             vmem_limit_bytes=VMEM_LIMIT_BYTES),
        name="mixer_out",
    )(att, a_out, gb, x, mod3, w_branch_b, w_out, g2, w_router, b_router)


def _row_copies(src_ref, dst_ref, src_rows, dst_rows, r, sem):
    return pltpu.make_async_copy(src_ref.at[pl.ds(src_rows(r), 1), :], dst_ref.at[pl.ds(dst_rows(r), 1), :], sem)


def _dispatch_kernel(tail_ref, d1_ref, d2_ref, h_ref, xb_ref, zero_ref, sem, zero_sem):
    n = h_ref.shape[0]

    @pl.when(pl.program_id(0) == 0)
    def _():
        zero_ref[...] = jnp.zeros_like(zero_ref)
        zero_block = lambda e: pltpu.make_async_copy(
            zero_ref, xb_ref.at[pl.ds(pl.multiple_of(tail_ref[e], DISPATCH_BLOCK), DISPATCH_BLOCK), :], zero_sem)
        for phase in ("start", "wait"):
            for e in range(tail_ref.shape[0]):
                @pl.when(tail_ref[e] >= 0)
                def _(e=e, phase=phase):
                    getattr(zero_block(e), phase)()

    same = lambda r: r
    copies = [functools.partial(_row_copies, h_ref, xb_ref, same, lambda r, d=d: d[0, 0, r])
              for d in (d1_ref, d2_ref)]

    def start(r, carry):
        for cp in copies:
            cp(r, sem).start()
        return carry

    lax.fori_loop(0, n, start, 0, unroll=ROW_DMA_UNROLL)
    for _ in copies:
        pltpu.make_async_copy(h_ref, xb_ref.at[pl.ds(0, n), :], sem).wait()


def _dispatch(h2, dest1, dest2, tail_block_start, n_rows):
    T, D = h2.shape
    n = ROW_DMA_TILE
    idx = pl.BlockSpec((1, 1, n), lambda i, tail: (i, 0, 0), memory_space=pltpu.SMEM)
    grid_spec = pltpu.PrefetchScalarGridSpec(
        num_scalar_prefetch=1,
        grid=(T // n,),
        in_specs=[idx, idx, pl.BlockSpec((n, D), lambda i, tail: (i, 0))],
        out_specs=pl.BlockSpec(memory_space=pl.ANY),
        scratch_shapes=[pltpu.VMEM((DISPATCH_BLOCK, D), F32), pltpu.SemaphoreType.DMA(()),
                        pltpu.SemaphoreType.DMA(())],
    )
    return pl.pallas_call(
        _dispatch_kernel,
        grid_spec=grid_spec,
        out_shape=jax.ShapeDtypeStruct((n_rows, D), F32),
        compiler_params=pltpu.CompilerParams(dimension_semantics=("arbitrary",)),
        name="dispatch",
    )(tail_block_start, dest1.reshape(T // n, 1, n), dest2.reshape(T // n, 1, n), h2)


def _expert_kernel(be_ref, nu_ref, x_ref, wg_ref, wu_ref, wd_ref, y_ref, wgu_bf, wd_bf):
    i = pl.program_id(0)
    used = i < nu_ref[0]

    @pl.when(used & ((i == 0) | (be_ref[i] != be_ref[jnp.maximum(i - 1, 0)])))
    def _():
        wgu_bf[:, :D_EXPERT] = wg_ref[0].astype(BF16)
        wgu_bf[:, D_EXPERT:] = wu_ref[0].astype(BF16)
        wd_bf[...] = wd_ref[0].astype(BF16)

    @pl.when(used)
    def _():
        gate_up = jnp.dot(x_ref[...].astype(BF16), wgu_bf[...], preferred_element_type=F32)
        hid = (jax.nn.silu(gate_up[:, :D_EXPERT]) * gate_up[:, D_EXPERT:]).astype(BF16)
        y_ref[...] = jnp.dot(hid, wd_bf[...], preferred_element_type=F32)

    @pl.when(i >= nu_ref[0])
    def _():
        y_ref[...] = jnp.zeros_like(y_ref)


def _experts(x_buf, block_expert, n_used, w_gate, w_up, w_down):
    P, D = x_buf.shape
    n_blocks = P // DISPATCH_BLOCK
    grid_spec = pltpu.PrefetchScalarGridSpec(
        num_scalar_prefetch=2,
        grid=(n_blocks,),
        in_specs=[pl.BlockSpec((DISPATCH_BLOCK, D), lambda i, be, nu: (jnp.minimum(i, nu[0] - 1), 0)),
                  pl.BlockSpec((1, D, D_EXPERT), lambda i, be, nu: (be[i], 0, 0)),
                  pl.BlockSpec((1, D, D_EXPERT), lambda i, be, nu: (be[i], 0, 0)),
                  pl.BlockSpec((1, D_EXPERT, D), lambda i, be, nu: (be[i], 0, 0))],
        out_specs=pl.BlockSpec((DISPATCH_BLOCK, D), lambda i, be, nu: (i, 0)),
        scratch_shapes=[pltpu.VMEM((D, 2 * D_EXPERT), BF16), pltpu.VMEM((D_EXPERT, D), BF16)],
    )
    return pl.pallas_call(
        _expert_kernel,
        grid_spec=grid_spec,
        out_shape=jax.ShapeDtypeStruct((P, D), F32),
        compiler_params=pltpu.CompilerParams(dimension_semantics=("arbitrary",),
                                             vmem_limit_bytes=VMEM_LIMIT_BYTES),
        name="experts",
    )(block_expert, n_used, x_buf, w_gate, w_up, w_down)


def _combine_kernel(d1_ref, d2_ref, d1n_ref, d2n_ref, x1_ref, rw_ref, mod_ref, fg_ref, yb_ref, o_ref, y_ref, sem):
    n = x1_ref.shape[1]
    step = pl.program_id(0) * pl.num_programs(1) + pl.program_id(1)
    last = pl.num_programs(0) * pl.num_programs(1) - 1
    slot = step % 2

    def gather(dests, buf):
        def start(r, carry):
            for k, d in enumerate(dests):
                pltpu.make_async_copy(yb_ref.at[pl.ds(d[0, 0, r], 1), :], y_ref.at[buf, k, pl.ds(r, 1), :],
                                      sem.at[buf]).start()
            return carry
        lax.fori_loop(0, n, start, 0, unroll=ROW_DMA_UNROLL)

    @pl.when(step == 0)
    def _():
        gather((d1_ref, d2_ref), 0)

    @pl.when(step < last)
    def _():
        gather((d1n_ref, d2n_ref), 1 - slot)

    for k in range(TOP_K_EXPERT):
        pltpu.make_async_copy(yb_ref.at[pl.ds(0, n), :], y_ref.at[slot, k], sem.at[slot]).wait()

    rw = rw_ref[0]
    moe = rw[:, 0:1] * y_ref[slot, 0] + rw[:, 1:2] * y_ref[slot, 1]
    g2 = mod_ref[0, 5:6, :]
    o_ref[0] = _rms(x1_ref[0] + g2 * moe) * fg_ref[...]


def _combine(x1, route_w, mod3, final_g, y_buf, dest1, dest2):
    B, S, D = x1.shape
    n = ROW_DMA_TILE
    nt = S // n
    idx = pl.BlockSpec((1, 1, n), lambda b, i: (b * nt + i, 0, 0), memory_space=pltpu.SMEM)
    idx_next = pl.BlockSpec((1, 1, n), lambda b, i: (jnp.minimum(b * nt + i + 1, B * nt - 1), 0, 0),
                            memory_space=pltpu.SMEM)
    tok = lambda w: pl.BlockSpec((1, n, w), lambda b, i: (b, i, 0))
    d1 = dest1.reshape(B * nt, 1, n)
    d2 = dest2.reshape(B * nt, 1, n)
    return pl.pallas_call(
        _combine_kernel,
        grid=(B, nt),
        in_specs=[idx, idx, idx_next, idx_next, tok(D), tok(LANES),
                  pl.BlockSpec((1, N_MOD, D), lambda b, i: (b, 0, 0)),
                  pl.BlockSpec((1, D), lambda b, i: (0, 0)),
                  pl.BlockSpec(memory_space=pl.ANY)],
        out_specs=tok(D),
        out_shape=jax.ShapeDtypeStruct((B, S, D), F32),
        scratch_shapes=[pltpu.VMEM((2, TOP_K_EXPERT, n, D), F32), pltpu.SemaphoreType.DMA((2,))],
        compiler_params=pltpu.CompilerParams(dimension_semantics=("arbitrary", "arbitrary")),
        name="combine",
    )(d1, d2, d1, d2, x1, route_w, mod3, final_g, y_buf)


def _rope_tables(S):
    half = ROPE_DIM // 2
    inv_freq = jnp.power(ROPE_THETA, -jnp.arange(half, dtype=jnp.float32) * 2.0 / ROPE_DIM)
    ang = jnp.arange(S).astype(jnp.float32)[:, None] * inv_freq[None, :]
    cos = jnp.cos(ang)
    sin = jnp.sin(ang)
    rest = HEAD_DIM - ROPE_DIM
    ones = jnp.ones((S, rest), F32)
    zeros = jnp.zeros((S, rest), F32)
    zh = jnp.zeros((S, half), F32)
    per_head = lambda parts: jnp.tile(jnp.concatenate(parts, axis=1), (1, HEADS_PER_LANE_TILE))
    return per_head([cos, cos, ones]), per_head([-sin, zh, zeros]), per_head([zh, sin, zeros])


def kernel(x, c, ada_w, ada_b, norm1_g, norm2_g, w_in, gmlp_ln_g, gmlp_ln_b, w_spatial, b_spatial, w_branch_a,
           w_branch_b, w_out, w_router_group, b_router_group, w_router_expert, b_router_expert, w_gate, w_up,
           w_down, final_norm_g):
    B, S, D = x.shape
    T = B * S
    assert D == D_MODEL and S % TOKEN_TILE == 0 and S % MOBA_BLOCK == 0 and S // MOBA_BLOCK <= 32
    assert ada_w.shape[0] == 1, "single layer"
    cos_t, sin_a, sin_b = _rope_tables(S)

    mod3 = _ada_mod(c, ada_w[0], ada_b).reshape(B, N_MOD, D)

    bias_sp = jnp.repeat(b_spatial[0].T, GMLP_CH, axis=1)
    q, k, v, a_out, gb, kmean = _mixer_in(
        x, mod3, norm1_g, w_in[0].astype(BF16), w_spatial[0], bias_sp, gmlp_ln_g, gmlp_ln_b,
        w_branch_a[0].astype(BF16), cos_t, sin_a, sin_b)
    att = _moba(q, k, v, kmean.reshape(B, S // MOBA_BLOCK, ATT_WIDTH))

    w_router = jnp.zeros((D, LANES), F32)
    w_router = w_router.at[:, :N_EXPERTS].set(w_router_expert[0].reshape(D, N_EXPERTS))
    w_router = w_router.at[:, ROUTER_GROUP_LANE0:ROUTER_GROUP_LANE0 + N_GROUPS].set(w_router_group[0])
    w_router_hi = w_router.astype(BF16)
    w_router = jnp.concatenate([w_router_hi, (w_router - w_router_hi.astype(F32)).astype(BF16)], axis=1)
    b_router = jnp.zeros((1, LANES), F32)
    b_router = b_router.at[0, :N_EXPERTS].set(b_router_expert[0].reshape(N_EXPERTS))
    b_router = b_router.at[0, ROUTER_GROUP_LANE0:ROUTER_GROUP_LANE0 + N_GROUPS].set(b_router_group[0])
    x1, h2, route_i, route_w, counts = _mixer_out(
        att, a_out, gb, x, mod3, w_branch_b[0].astype(BF16), w_out[0].astype(BF16), norm2_g, w_router, b_router)

    counts = counts[0, :N_EXPERTS].astype(jnp.int32)
    padded = (counts + DISPATCH_BLOCK - 1) // DISPATCH_BLOCK * DISPATCH_BLOCK
    pend = jnp.cumsum(padded)
    pstart = pend - padded
    n_blocks = -(-(T * TOP_K_EXPERT) // DISPATCH_BLOCK) + N_EXPERTS
    block_start = jnp.arange(n_blocks, dtype=jnp.int32) * DISPATCH_BLOCK
    block_expert = jnp.minimum(
        jnp.sum((pend[None, :] <= block_start[:, None]).astype(jnp.int32), axis=1), N_EXPERTS - 1)
    n_used = (pend[-1:] // DISPATCH_BLOCK).astype(jnp.int32)
    expert_ids = jnp.arange(N_EXPERTS, dtype=jnp.int32)

    def slot(expert, rank):
        return jnp.sum(jnp.where(expert[:, None] == expert_ids[None, :], pstart[None, :], 0), axis=1) + rank

    dest1 = slot(route_i[0], route_i[2])
    dest2 = slot(route_i[1], route_i[3])

    spare = (n_used[0] + jnp.arange(N_EXPERTS, dtype=jnp.int32)) * DISPATCH_BLOCK
    tail_block_start = jnp.concatenate([
        jnp.where(counts > 0, pend - DISPATCH_BLOCK, -1),
        jnp.where(spare < n_blocks * DISPATCH_BLOCK, spare, -1)]).astype(jnp.int32)
    x_buf = _dispatch(h2.reshape(T, D), dest1, dest2, tail_block_start, n_blocks * DISPATCH_BLOCK)
    y_buf = _experts(x_buf, block_expert, n_used, w_gate[0], w_up[0], w_down[0])
    return _combine(x1, route_w, mod3, final_norm_g.reshape(1, D), y_buf, dest1, dest2)
```

```python
import functools

import jax
import jax.numpy as jnp
from jax import lax
from jax.experimental import pallas as pl
from jax.experimental.pallas import tpu as pltpu

D_MODEL = 1024
GMLP_WIDTH = D_MODEL // 2
GMLP_GROUPS = 4
GMLP_CH = GMLP_WIDTH // GMLP_GROUPS
GMLP_CHUNK = 128
ATT_HEADS = 8
HEAD_DIM = 64
ATT_WIDTH = ATT_HEADS * HEAD_DIM
ROPE_DIM = HEAD_DIM // 4
ROPE_THETA = 500000.0
MOBA_BLOCK = 256
MOBA_TOPK = 3
N_GROUPS = 4
EXPERTS_PER_GROUP = 8
N_EXPERTS = N_GROUPS * EXPERTS_PER_GROUP
TOP_K_EXPERT = 2
D_EXPERT = D_MODEL // 2
DISPATCH_BLOCK = 256
N_MOD = 6
EPS = 1e-6

LANES = 128
HEADS_PER_LANE_TILE = LANES // HEAD_DIM
VMEM_LIMIT_BYTES = 56 * 1024 * 1024

TOKEN_TILE = 512
ROW_DMA_TILE = 256
ROW_DMA_UNROLL = 8

F32 = jnp.float32
BF16 = jnp.bfloat16
NEG_INF = float("-inf")
LOG2_E = 1.4426950408889634


def _rms(x):
    return x * lax.rsqrt(jnp.mean(x * x, axis=-1, keepdims=True) + EPS)


def _first_index_of(mask, idx, sentinel):
    return jnp.min(jnp.where(mask, idx, sentinel), axis=1, keepdims=True)


def _ada_kernel(c_ref, w_ref, b_ref, o_ref):
    o_ref[...] = jnp.dot(c_ref[...], w_ref[...], preferred_element_type=F32,
                         precision=lax.Precision.HIGHEST) + b_ref[...]


def _ada_mod(c, ada_w, ada_b):
    B = c.shape[0]
    return pl.pallas_call(
        _ada_kernel,
        grid=(N_MOD,),
        in_specs=[pl.BlockSpec((B, D_MODEL), lambda j: (0, 0)),
                  pl.BlockSpec((D_MODEL, D_MODEL), lambda j: (0, j)),
                  pl.BlockSpec((1, D_MODEL), lambda j: (0, j))],
        out_specs=pl.BlockSpec((B, D_MODEL), lambda j: (0, j)),
        out_shape=jax.ShapeDtypeStruct((B, N_MOD * D_MODEL), F32),
        name="ada_mod",
    )(c, ada_w, ada_b)


def _mixer_in_kernel(x_ref, mod_ref, g1_ref, w_in_ref, wsp_ref, bsp_ref, lng_ref, lnb_ref, wba_ref,
                     cos_ref, sa_ref, sb_ref,
                     q_ref, k_ref, v_ref, a_ref, gb_ref, km_ref):
    tm = x_ref.shape[1]
    sh1 = mod_ref[0, 0:1, :]
    sc1 = mod_ref[0, 1:2, :]
    h = (_rms(x_ref[0]) * g1_ref[...]) * (1.0 + sc1) + sh1
    hb = h.astype(BF16)

    off = 2 * GMLP_WIDTH + 3 * ATT_WIDTH
    pa = jnp.dot(hb, w_in_ref[:, 0:2 * GMLP_WIDTH], preferred_element_type=F32)
    gbv = jnp.dot(hb, w_in_ref[:, off + D_MODEL:off + 2 * D_MODEL], preferred_element_type=F32)
    qkv = jnp.dot(hb, w_in_ref[:, 2 * GMLP_WIDTH:2 * GMLP_WIDTH + 3 * ATT_WIDTH], preferred_element_type=F32)
    ga = jnp.dot(hb, w_in_ref[:, off:off + D_MODEL], preferred_element_type=F32)
    gb_ref[0] = jax.nn.sigmoid(gbv).astype(BF16)

    z = jax.nn.gelu(pa)
    u = z[:, :GMLP_WIDTH]
    v = z[:, GMLP_WIDTH:]
    mu = jnp.mean(v, axis=-1, keepdims=True)
    var = jnp.mean(jnp.square(v - mu), axis=-1, keepdims=True)
    vn = ((v - mu) * lax.rsqrt(var + EPS) * lng_ref[...] + lnb_ref[...]).astype(BF16)
    t_idx = lax.broadcasted_iota(jnp.int32, (GMLP_CHUNK, GMLP_CHUNK), 0)
    s_idx = lax.broadcasted_iota(jnp.int32, (GMLP_CHUNK, GMLP_CHUNK), 1)
    w_causal = [jnp.where(t_idx >= s_idx, wsp_ref[g], 0.0).astype(BF16) for g in range(GMLP_GROUPS)]
    chunks = []
    for c in range(tm // GMLP_CHUNK):
        rows = slice(c * GMLP_CHUNK, (c + 1) * GMLP_CHUNK)
        cols = [jnp.dot(w_causal[g], vn[rows, g * GMLP_CH:(g + 1) * GMLP_CH], preferred_element_type=F32)
                for g in range(GMLP_GROUPS)]
        chunks.append(jnp.concatenate(cols, axis=1) + bsp_ref[...])
    sv = jnp.concatenate(chunks, axis=0)
    ya = jnp.dot((u * sv).astype(BF16), wba_ref[...], preferred_element_type=F32)

    a_ref[0] = (jax.nn.sigmoid(ga) * ya).astype(BF16)

    cos_t = cos_ref[...]
    sin_a = sa_ref[...]
    sin_b = sb_ref[...]
    half = ROPE_DIM // 2

    def rope(t):
        outs = []
        for j in range(ATT_WIDTH // LANES):
            tj = t[:, j * LANES:(j + 1) * LANES]
            outs.append(tj * cos_t + pltpu.roll(tj, LANES - half, 1) * sin_a + pltpu.roll(tj, half, 1) * sin_b)
        return jnp.concatenate(outs, axis=1)

    q = rope(qkv[:, :ATT_WIDTH]) * (HEAD_DIM ** -0.5 * LOG2_E)
    k = rope(qkv[:, ATT_WIDTH:2 * ATT_WIDTH])
    q_ref[0] = q.astype(BF16)
    k_ref[0] = k.astype(BF16)
    v_ref[0] = qkv[:, 2 * ATT_WIDTH:].T.astype(BF16)
    for r in range(tm // MOBA_BLOCK):
        km_ref[0, 0, r:r + 1, :] = jnp.mean(k[r * MOBA_BLOCK:(r + 1) * MOBA_BLOCK], axis=0, keepdims=True)


def _const_spec(shape):
    nd = len(shape)
    return pl.BlockSpec(shape, lambda *_: (0,) * nd, pipeline_mode=pl.Buffered(1))


def _mixer_in(x, mod3, g1, w_in, w_spatial, bias_sp, ln_g, ln_b, w_branch_a, cos_t, sin_a, sin_b):
    B, S, D = x.shape
    tm = TOKEN_TILE
    nt = S // tm
    tok = lambda w: pl.BlockSpec((1, tm, w), lambda b, i: (b, i, 0))
    tab = pl.BlockSpec((tm, LANES), lambda b, i: (i, 0))
    return pl.pallas_call(
        _mixer_in_kernel,
        grid=(B, nt),
        in_specs=[tok(D),
                  pl.BlockSpec((1, N_MOD, D), lambda b, i: (b, 0, 0)),
                  _const_spec((1, D)),
                  _const_spec(w_in.shape),
                  _const_spec(w_spatial.shape),
                  _const_spec(bias_sp.shape),
                  _const_spec((1, GMLP_WIDTH)),
                  _const_spec((1, GMLP_WIDTH)),
                  _const_spec(w_branch_a.shape),
                  tab, tab, tab],
        out_specs=[tok(ATT_WIDTH), tok(ATT_WIDTH),
                   pl.BlockSpec((1, ATT_WIDTH, tm), lambda b, i: (b, 0, i)), tok(D), tok(D),
                   pl.BlockSpec((1, 1, tm // MOBA_BLOCK, ATT_WIDTH), lambda b, i: (b, i, 0, 0))],
        out_shape=[jax.ShapeDtypeStruct((B, S, ATT_WIDTH), BF16)] * 2
                  + [jax.ShapeDtypeStruct((B, ATT_WIDTH, S), BF16)]
                  + [jax.ShapeDtypeStruct((B, S, D), BF16)] * 2
                  + [jax.ShapeDtypeStruct((B, nt, tm // MOBA_BLOCK, ATT_WIDTH), F32)],
        compiler_params=pltpu.CompilerParams(dimension_semantics=("arbitrary", "arbitrary"),
                                             vmem_limit_bytes=VMEM_LIMIT_BYTES),
        name="mixer_in",
    )(x, mod3, g1, w_in, w_spatial, bias_sp, ln_g, ln_b, w_branch_a, cos_t, sin_a, sin_b)


MOBA_DENOM_ROWS = 16


def _moba_kernel(q_ref, k_ref, vt_ref, km_ref, o_ref, *scratch):
    tq = MOBA_BLOCK

    def tile(i, carry):
        rows = pl.ds(pl.multiple_of(i * tq, tq), tq)
        o_ref[0, rows, :] = _moba_tile(i, q_ref[0, rows, :], k_ref, vt_ref, km_ref, *scratch)
        return carry

    lax.fori_loop(0, q_ref.shape[1] // tq, tile, 0)


def _moba_tile(i, q2, k_ref, vt_ref, km_ref, m_ref, acc_ref, bits_ref,
               s0_ref, s1_ref, p0_ref, p1_ref, a0_ref, a1_ref):
    tq = q2.shape[0]
    nb = km_ref.shape[1]
    n_cols = HEADS_PER_LANE_TILE * tq
    q_t = q2.astype(F32).T
    feat = lax.broadcasted_iota(jnp.int32, (LANES, tq), 0)
    qs_t = jnp.concatenate([jnp.where(feat < HEAD_DIM, q_t, 0.0), jnp.where(feat >= HEAD_DIM, q_t, 0.0)],
                           axis=1).astype(BF16)

    km = km_ref[0]
    km_hi = km.astype(BF16)
    km_lo = (km - km_hi.astype(F32)).astype(BF16)
    gate = (jnp.dot(km_hi, qs_t, preferred_element_type=F32)
            + jnp.dot(km_lo, qs_t, preferred_element_type=F32))
    n_idx = lax.broadcasted_iota(jnp.int32, (nb, n_cols), 0)
    g = jnp.where(n_idx < i, gate, NEG_INF)
    bits = jnp.zeros((1, n_cols), jnp.int32)
    for r in range(MOBA_TOPK):
        mx = jnp.max(g, axis=0, keepdims=True)
        first = jnp.min(jnp.where(g == mx, n_idx, nb), axis=0, keepdims=True)
        bits = bits | jnp.where(r < i, jnp.left_shift(1, first), 0)
        g = jnp.where(n_idx == first, NEG_INF, g)
    bits_ref[...] = bits

    def block_start(t):
        return pl.multiple_of(jnp.where(t == 0, i, t - 1) * MOBA_BLOCK, MOBA_BLOCK)

    def scores(t, s_ref):
        t = jnp.minimum(t, i)
        s_ref[...] = jnp.dot(k_ref[0, pl.ds(block_start(t), MOBA_BLOCK), :], qs_t, preferred_element_type=F32)

    def softmax_past(t, s_ref, p_ref, alpha_ref):
        j = jnp.minimum(t - 1, nb - 1)
        s = s_ref[...]
        picked = (lax.shift_right_logical(bits_ref[...], j) & 1) == 1
        m_old = m_ref[...]
        m_new = jnp.where(picked, jnp.maximum(m_old, jnp.max(s, axis=0, keepdims=True)), m_old)
        alpha_ref[...] = jnp.exp2(m_old - m_new)
        m_sub = jnp.where(picked, m_new, jnp.inf)
        p_ref[...] = jnp.exp2(s - m_sub).astype(BF16)
        m_ref[...] = m_new

    def accumulate(t, p_ref, alpha_ref):
        vt = vt_ref[0, :, pl.ds(block_start(t), MOBA_BLOCK)]
        ones = jnp.ones((MOBA_DENOM_ROWS, MOBA_BLOCK), BF16)
        for h in range(HEADS_PER_LANE_TILE):
            cols = slice(h * tq, (h + 1) * tq)
            v_h = jnp.concatenate([vt[h * HEAD_DIM:(h + 1) * HEAD_DIM], ones], axis=0)
            acc_ref[h] = alpha_ref[:, cols] * acc_ref[h] + jnp.dot(v_h, p_ref[:, cols], preferred_element_type=F32)

    s_refs, p_refs, alpha_refs = (s0_ref, s1_ref), (p0_ref, p1_ref), (a0_ref, a1_ref)

    scores(0, s_refs[0])
    key = lax.broadcasted_iota(jnp.int32, (MOBA_BLOCK, LANES), 0)
    qry = lax.broadcasted_iota(jnp.int32, (MOBA_BLOCK, LANES), 1)
    for c in range(n_cols // LANES):
        cols = slice(c * LANES, (c + 1) * LANES)
        s = jnp.where(key <= qry + (c * LANES) % tq, s_refs[0][:, cols], NEG_INF)
        m0 = jnp.max(s, axis=0, keepdims=True)
        p_refs[0][:, cols] = jnp.exp2(s - m0).astype(BF16)
        m_ref[:, cols] = m0
    alpha_refs[0][...] = jnp.ones_like(alpha_refs[0])
    acc_ref[...] = jnp.zeros_like(acc_ref)
    scores(1, s_refs[1])

    def tick_pair(d, carry):
        for half in range(2):
            t = 2 * d + 1 + half
            cur, prev = (1 - half), half
            scores(t + 1, s_refs[prev])
            softmax_past(t, s_refs[cur], p_refs[cur], alpha_refs[cur])
            accumulate(t - 1, p_refs[prev], alpha_refs[prev])
        return carry

    lax.fori_loop(0, i // 2, tick_pair, 0)

    @pl.when(i % 2 == 1)
    def _():
        accumulate(i - 1, p_refs[0], alpha_refs[0])
        softmax_past(i, s_refs[1], p_refs[1], alpha_refs[1])
        accumulate(i, p_refs[1], alpha_refs[1])

    @pl.when(i % 2 == 0)
    def _():
        accumulate(i, p_refs[0], alpha_refs[0])

    outs = [acc_ref[h, :HEAD_DIM, :] / acc_ref[h, HEAD_DIM:HEAD_DIM + 1, :] for h in range(HEADS_PER_LANE_TILE)]
    return jnp.concatenate(outs, axis=0).T.astype(BF16)


def _moba(q, k, vt, kmean):
    B, S, _ = q.shape
    tq = MOBA_BLOCK
    nb = S // MOBA_BLOCK
    n_pairs = ATT_HEADS // HEADS_PER_LANE_TILE
    n_cols = HEADS_PER_LANE_TILE * tq
    seq = pl.BlockSpec((1, S, LANES), lambda b, hp: (b, 0, hp))
    return pl.pallas_call(
        _moba_kernel,
        grid=(B, n_pairs),
        in_specs=[seq, seq,
                  pl.BlockSpec((1, LANES, S), lambda b, hp: (b, hp, 0)),
                  pl.BlockSpec((1, nb, LANES), lambda b, hp: (b, 0, hp))],
        out_specs=seq,
        out_shape=jax.ShapeDtypeStruct((B, S, ATT_WIDTH), BF16),
        scratch_shapes=[pltpu.VMEM((1, n_cols), F32),
                        pltpu.VMEM((HEADS_PER_LANE_TILE, HEAD_DIM + MOBA_DENOM_ROWS, tq), F32),
                        pltpu.VMEM((1, n_cols), jnp.int32),
                        pltpu.VMEM((MOBA_BLOCK, n_cols), F32),
                        pltpu.VMEM((MOBA_BLOCK, n_cols), F32),
                        pltpu.VMEM((MOBA_BLOCK, n_cols), BF16),
                        pltpu.VMEM((MOBA_BLOCK, n_cols), BF16),
                        pltpu.VMEM((1, n_cols), F32),
                        pltpu.VMEM((1, n_cols), F32)],
        compiler_params=pltpu.CompilerParams(dimension_semantics=("arbitrary", "arbitrary"),
                                             vmem_limit_bytes=VMEM_LIMIT_BYTES),
        name="moba",
    )(q, k, vt, kmean)


ROUTER_GROUP_LANE0 = N_EXPERTS
ROUTE_ROWS = 8


def _mixer_out_kernel(att_ref, a_ref, gb_ref, x_ref, mod_ref, wbb_ref, wout_ref, g2_ref, wr_ref, br_ref,
                      x1_ref, h2_ref, ri_ref, rw_ref, cnt_ref, run_ref):
    tm = x_ref.shape[1]

    @pl.when((pl.program_id(0) == 0) & (pl.program_id(1) == 0))
    def _():
        run_ref[...] = jnp.zeros_like(run_ref)

    yb = jnp.dot(att_ref[0], wbb_ref[...], preferred_element_type=F32)
    merged = a_ref[0].astype(F32) + gb_ref[0].astype(F32) * yb
    mo = jnp.dot(merged.astype(BF16), wout_ref[...], preferred_element_type=F32)
    g1 = mod_ref[0, 2:3, :]
    sh2 = mod_ref[0, 3:4, :]
    sc2 = mod_ref[0, 4:5, :]
    x1 = x_ref[0] + g1 * mo
    x1_ref[0] = x1
    h2 = (_rms(x1) * g2_ref[...]) * (1.0 + sc2) + sh2
    h2_ref[0] = h2

    h_hi = h2.astype(BF16)
    h_lo = (h2 - h_hi.astype(F32)).astype(BF16)
    parts = jnp.dot(jnp.concatenate([h_hi, h_lo], axis=0), wr_ref[...], preferred_element_type=F32)
    logits = (parts[:tm, :LANES] + parts[tm:, :LANES] + parts[:tm, LANES:]) + br_ref[...]
    lane = lax.broadcasted_iota(jnp.int32, (tm, LANES), 1)
    is_group = (lane >= ROUTER_GROUP_LANE0) & (lane < ROUTER_GROUP_LANE0 + N_GROUPS)
    gl = jnp.where(is_group, logits, NEG_INF)
    ge = jnp.exp(gl - jnp.max(gl, axis=1, keepdims=True))
    g_prob = jnp.where(is_group, ge / jnp.sum(ge, axis=1, keepdims=True), -1.0)
    g_p = jnp.max(g_prob, axis=1, keepdims=True)
    lane_f = lane.astype(F32)
    g_idx = _first_index_of(g_prob == g_p, lane_f, float(LANES)) - float(ROUTER_GROUP_LANE0)
    in_group = (lane_f >= g_idx * EXPERTS_PER_GROUP) & (lane_f < (g_idx + 1.0) * EXPERTS_PER_GROUP)
    el = jnp.where(in_group, logits, NEG_INF)
    ee = jnp.exp(el - jnp.max(el, axis=1, keepdims=True))
    e_prob = jnp.where(in_group, ee / jnp.sum(ee, axis=1, keepdims=True), -1.0)
    p1 = jnp.max(e_prob, axis=1, keepdims=True)
    i1 = _first_index_of(e_prob == p1, lane_f, float(LANES))
    e_prob2 = jnp.where(lane_f == i1, -1.0, e_prob)
    p2 = jnp.max(e_prob2, axis=1, keepdims=True)
    i2 = _first_index_of(e_prob2 == p2, lane_f, float(LANES))
    w1 = g_p * p1 / (p1 + p2)
    w2 = g_p * p2 / (p1 + p2)

    hit1 = lane_f == i1
    hit2 = lane_f == i2
    onehot = jnp.where(hit1 | hit2, 1.0, 0.0)
    r_i = lax.broadcasted_iota(jnp.int32, (tm, tm), 0)
    c_i = lax.broadcasted_iota(jnp.int32, (tm, tm), 1)
    earlier = jnp.where(r_i > c_i, 1.0, 0.0).astype(BF16)
    cum = jnp.dot(earlier, onehot.astype(BF16), preferred_element_type=F32) + run_ref[...]
    rank1 = jnp.sum(jnp.where(hit1, cum, 0.0), axis=1, keepdims=True).astype(jnp.int32)
    rank2 = jnp.sum(jnp.where(hit2, cum, 0.0), axis=1, keepdims=True).astype(jnp.int32)
    run_ref[...] = run_ref[...] + jnp.sum(onehot, axis=0, keepdims=True)
    cnt_ref[...] = run_ref[...]

    ri = jnp.where(lane == 0, i1.astype(jnp.int32), jnp.where(lane == 1, i2.astype(jnp.int32),
                   jnp.where(lane == 2, rank1, jnp.where(lane == 3, rank2, 0))))
    ri_ref[...] = ri.T[:ROUTE_ROWS, :]
    rw_ref[0] = jnp.where(lane == 0, w1, jnp.where(lane == 1, w2, 0.0))


def _mixer_out(att, a_out, gb, x, mod3, w_branch_b, w_out, g2, w_router, b_router):
    B, S, D = x.shape
    tm = TOKEN_TILE
    tok = lambda w: pl.BlockSpec((1, tm, w), lambda b, i: (b, i, 0))
    return pl.pallas_call(
        _mixer_out_kernel,
        grid=(B, S // tm),
        in_specs=[tok(ATT_WIDTH), tok(D), tok(D), tok(D),
                  pl.BlockSpec((1, N_MOD, D), lambda b, i: (b, 0, 0)),
                  _const_spec(w_branch_b.shape),
                  _const_spec(w_out.shape),
                  _const_spec((1, D)),
                  _const_spec(w_router.shape),
                  _const_spec((1, LANES))],
        out_specs=[tok(D), tok(D),
                   pl.BlockSpec((ROUTE_ROWS, tm), lambda b, i: (0, b * (S // tm) + i)), tok(LANES),
                   pl.BlockSpec((1, LANES), lambda b, i: (0, 0))],
        out_shape=[jax.ShapeDtypeStruct((B, S, D), F32),
                   jax.ShapeDtypeStruct((B, S, D), F32),
                   jax.ShapeDtypeStruct((ROUTE_ROWS, B * S), jnp.int32),
                   jax.ShapeDtypeStruct((B, S, LANES), F32),
                   jax.ShapeDtypeStruct((1, LANES), F32)],
        scratch_shapes=[pltpu.VMEM((1, LANES), F32)],
        compiler_params=pltpu.CompilerParams(dimension_semantics=("arbitrary", "arbitrary"),
                                             vmem_limit_bytes=VMEM_LIMIT_BYTES),
        name="mixer_out",
    )(att, a_out, gb, x, mod3, w_branch_b, w_out, g2, w_router, b_router)


def _row_copies(src_ref, dst_ref, src_rows, dst_rows, r, sem):
    return pltpu.make_async_copy(src_ref.at[pl.ds(src_rows(r), 1), :], dst_ref.at[pl.ds(dst_rows(r), 1), :], sem)


def _dispatch_kernel(tail_ref, d1_ref, d2_ref, h_ref, xb_ref, zero_ref, sem, zero_sem):
    n = h_ref.shape[0]

    @pl.when(pl.program_id(0) == 0)
    def _():
        zero_ref[...] = jnp.zeros_like(zero_ref)
        zero_block = lambda e: pltpu.make_async_copy(
            zero_ref, xb_ref.at[pl.ds(pl.multiple_of(tail_ref[e], DISPATCH_BLOCK), DISPATCH_BLOCK), :], zero_sem)
        for phase in ("start", "wait"):
            for e in range(tail_ref.shape[0]):
                @pl.when(tail_ref[e] >= 0)
                def _(e=e, phase=phase):
                    getattr(zero_block(e), phase)()

    same = lambda r: r
    copies = [functools.partial(_row_copies, h_ref, xb_ref, same, lambda r, d=d: d[0, 0, r])
              for d in (d1_ref, d2_ref)]

    def start(r, carry):
        for cp in copies:
            cp(r, sem).start()
        return carry

    lax.fori_loop(0, n, start, 0, unroll=ROW_DMA_UNROLL)
    for _ in copies:
        pltpu.make_async_copy(h_ref, xb_ref.at[pl.ds(0, n), :], sem).wait()


def _dispatch(h2, dest1, dest2, tail_block_start, n_rows):
    T, D = h2.shape
    n = ROW_DMA_TILE
    idx = pl.BlockSpec((1, 1, n), lambda i, tail: (i, 0, 0), memory_space=pltpu.SMEM)
    grid_spec = pltpu.PrefetchScalarGridSpec(
        num_scalar_prefetch=1,
        grid=(T // n,),
        in_specs=[idx, idx, pl.BlockSpec((n, D), lambda i, tail: (i, 0))],
        out_specs=pl.BlockSpec(memory_space=pl.ANY),
        scratch_shapes=[pltpu.VMEM((DISPATCH_BLOCK, D), F32), pltpu.SemaphoreType.DMA(()),
                        pltpu.SemaphoreType.DMA(())],
    )
    return pl.pallas_call(
        _dispatch_kernel,
        grid_spec=grid_spec,
        out_shape=jax.ShapeDtypeStruct((n_rows, D), F32),
        compiler_params=pltpu.CompilerParams(dimension_semantics=("arbitrary",)),
        name="dispatch",
    )(tail_block_start, dest1.reshape(T // n, 1, n), dest2.reshape(T // n, 1, n), h2)


def _expert_kernel(be_ref, nu_ref, x_ref, wg_ref, wu_ref, wd_ref, y_ref, wgu_bf, wd_bf):
    i = pl.program_id(0)
    used = i < nu_ref[0]

    @pl.when(used & ((i == 0) | (be_ref[i] != be_ref[jnp.maximum(i - 1, 0)])))
    def _():
        wgu_bf[:, :D_EXPERT] = wg_ref[0].astype(BF16)
        wgu_bf[:, D_EXPERT:] = wu_ref[0].astype(BF16)
        wd_bf[...] = wd_ref[0].astype(BF16)

    @pl.when(used)
    def _():
        gate_up = jnp.dot(x_ref[...].astype(BF16), wgu_bf[...], preferred_element_type=F32)
        hid = (jax.nn.silu(gate_up[:, :D_EXPERT]) * gate_up[:, D_EXPERT:]).astype(BF16)
        y_ref[...] = jnp.dot(hid, wd_bf[...], preferred_element_type=F32)

    @pl.when(i >= nu_ref[0])
    def _():
        y_ref[...] = jnp.zeros_like(y_ref)


def _experts(x_buf, block_expert, n_used, w_gate, w_up, w_down):
    P, D = x_buf.shape
    n_blocks = P // DISPATCH_BLOCK
    grid_spec = pltpu.PrefetchScalarGridSpec(
        num_scalar_prefetch=2,
        grid=(n_blocks,),
        in_specs=[pl.BlockSpec((DISPATCH_BLOCK, D), lambda i, be, nu: (jnp.minimum(i, nu[0] - 1), 0)),
                  pl.BlockSpec((1, D, D_EXPERT), lambda i, be, nu: (be[i], 0, 0)),
                  pl.BlockSpec((1, D, D_EXPERT), lambda i, be, nu: (be[i], 0, 0)),
                  pl.BlockSpec((1, D_EXPERT, D), lambda i, be, nu: (be[i], 0, 0))],
        out_specs=pl.BlockSpec((DISPATCH_BLOCK, D), lambda i, be, nu: (i, 0)),
        scratch_shapes=[pltpu.VMEM((D, 2 * D_EXPERT), BF16), pltpu.VMEM((D_EXPERT, D), BF16)],
    )
    return pl.pallas_call(
        _expert_kernel,
        grid_spec=grid_spec,
        out_shape=jax.ShapeDtypeStruct((P, D), F32),
        compiler_params=pltpu.CompilerParams(dimension_semantics=("arbitrary",),
                                             vmem_limit_bytes=VMEM_LIMIT_BYTES),
        name="experts",
    )(block_expert, n_used, x_buf, w_gate, w_up, w_down)


def _combine_kernel(d1_ref, d2_ref, d1n_ref, d2n_ref, x1_ref, rw_ref, mod_ref, fg_ref, yb_ref, o_ref, y_ref, sem):
    n = x1_ref.shape[1]
    step = pl.program_id(0) * pl.num_programs(1) + pl.program_id(1)
    last = pl.num_programs(0) * pl.num_programs(1) - 1
    slot = step % 2

    def gather(dests, buf):
        def start(r, carry):
            for k, d in enumerate(dests):
                pltpu.make_async_copy(yb_ref.at[pl.ds(d[0, 0, r], 1), :], y_ref.at[buf, k, pl.ds(r, 1), :],
                                      sem.at[buf]).start()
            return carry
        lax.fori_loop(0, n, start, 0, unroll=ROW_DMA_UNROLL)

    @pl.when(step == 0)
    def _():
        gather((d1_ref, d2_ref), 0)

    @pl.when(step < last)
    def _():
        gather((d1n_ref, d2n_ref), 1 - slot)

    for k in range(TOP_K_EXPERT):
        pltpu.make_async_copy(yb_ref.at[pl.ds(0, n), :], y_ref.at[slot, k], sem.at[slot]).wait()

    rw = rw_ref[0]
    moe = rw[:, 0:1] * y_ref[slot, 0] + rw[:, 1:2] * y_ref[slot, 1]
    g2 = mod_ref[0, 5:6, :]
    o_ref[0] = _rms(x1_ref[0] + g2 * moe) * fg_ref[...]


def _combine(x1, route_w, mod3, final_g, y_buf, dest1, dest2):
    B, S, D = x1.shape
    n = ROW_DMA_TILE
    nt = S // n
    idx = pl.BlockSpec((1, 1, n), lambda b, i: (b * nt + i, 0, 0), memory_space=pltpu.SMEM)
    idx_next = pl.BlockSpec((1, 1, n), lambda b, i: (jnp.minimum(b * nt + i + 1, B * nt - 1), 0, 0),
                            memory_space=pltpu.SMEM)
    tok = lambda w: pl.BlockSpec((1, n, w), lambda b, i: (b, i, 0))
    d1 = dest1.reshape(B * nt, 1, n)
    d2 = dest2.reshape(B * nt, 1, n)
    return pl.pallas_call(
        _combine_kernel,
        grid=(B, nt),
        in_specs=[idx, idx, idx_next, idx_next, tok(D), tok(LANES),
                  pl.BlockSpec((1, N_MOD, D), lambda b, i: (b, 0, 0)),
                  pl.BlockSpec((1, D), lambda b, i: (0, 0)),
                  pl.BlockSpec(memory_space=pl.ANY)],
        out_specs=tok(D),
        out_shape=jax.ShapeDtypeStruct((B, S, D), F32),
        scratch_shapes=[pltpu.VMEM((2, TOP_K_EXPERT, n, D), F32), pltpu.SemaphoreType.DMA((2,))],
        compiler_params=pltpu.CompilerParams(dimension_semantics=("arbitrary", "arbitrary")),
        name="combine",
    )(d1, d2, d1, d2, x1, route_w, mod3, final_g, y_buf)


def _rope_tables(S):
    half = ROPE_DIM // 2
    inv_freq = jnp.power(ROPE_THETA, -jnp.arange(half, dtype=jnp.float32) * 2.0 / ROPE_DIM)
    ang = jnp.arange(S).astype(jnp.float32)[:, None] * inv_freq[None, :]
    cos = jnp.cos(ang)
    sin = jnp.sin(ang)
    rest = HEAD_DIM - ROPE_DIM
    ones = jnp.ones((S, rest), F32)
    zeros = jnp.zeros((S, rest), F32)
    zh = jnp.zeros((S, half), F32)
    per_head = lambda parts: jnp.tile(jnp.concatenate(parts, axis=1), (1, HEADS_PER_LANE_TILE))
    return per_head([cos, cos, ones]), per_head([-sin, zh, zeros]), per_head([zh, sin, zeros])


def kernel(x, c, ada_w, ada_b, norm1_g, norm2_g, w_in, gmlp_ln_g, gmlp_ln_b, w_spatial, b_spatial, w_branch_a,
           w_branch_b, w_out, w_router_group, b_router_group, w_router_expert, b_router_expert, w_gate, w_up,
           w_down, final_norm_g):
    B, S, D = x.shape
    T = B * S
    assert D == D_MODEL and S % TOKEN_TILE == 0 and S % MOBA_BLOCK == 0 and S // MOBA_BLOCK <= 32
    assert ada_w.shape[0] == 1, "single layer"
    cos_t, sin_a, sin_b = _rope_tables(S)

    mod3 = _ada_mod(c, ada_w[0], ada_b).reshape(B, N_MOD, D)

    bias_sp = jnp.repeat(b_spatial[0].T, GMLP_CH, axis=1)
    q, k, v, a_out, gb, kmean = _mixer_in(
        x, mod3, norm1_g, w_in[0].astype(BF16), w_spatial[0], bias_sp, gmlp_ln_g, gmlp_ln_b,
        w_branch_a[0].astype(BF16), cos_t, sin_a, sin_b)
    att = _moba(q, k, v, kmean.reshape(B, S // MOBA_BLOCK, ATT_WIDTH))

    w_router = jnp.zeros((D, LANES), F32)
    w_router = w_router.at[:, :N_EXPERTS].set(w_router_expert[0].reshape(D, N_EXPERTS))
    w_router = w_router.at[:, ROUTER_GROUP_LANE0:ROUTER_GROUP_LANE0 + N_GROUPS].set(w_router_group[0])
    w_router_hi = w_router.astype(BF16)
    w_router = jnp.concatenate([w_router_hi, (w_router - w_router_hi.astype(F32)).astype(BF16)], axis=1)
    b_router = jnp.zeros((1, LANES), F32)
    b_router = b_router.at[0, :N_EXPERTS].set(b_router_expert[0].reshape(N_EXPERTS))
    b_router = b_router.at[0, ROUTER_GROUP_LANE0:ROUTER_GROUP_LANE0 + N_GROUPS].set(b_router_group[0])
    x1, h2, route_i, route_w, counts = _mixer_out(
        att, a_out, gb, x, mod3, w_branch_b[0].astype(BF16), w_out[0].astype(BF16), norm2_g, w_router, b_router)

    counts = counts[0, :N_EXPERTS].astype(jnp.int32)
    padded = (counts + DISPATCH_BLOCK - 1) // DISPATCH_BLOCK * DISPATCH_BLOCK
    pend = jnp.cumsum(padded)
    pstart = pend - padded
    n_blocks = -(-(T * TOP_K_EXPERT) // DISPATCH_BLOCK) + N_EXPERTS
    block_start = jnp.arange(n_blocks, dtype=jnp.int32) * DISPATCH_BLOCK
    block_expert = jnp.minimum(
        jnp.sum((pend[None, :] <= block_start[:, None]).astype(jnp.int32), axis=1), N_EXPERTS - 1)
    n_used = (pend[-1:] // DISPATCH_BLOCK).astype(jnp.int32)
    expert_ids = jnp.arange(N_EXPERTS, dtype=jnp.int32)

    def slot(expert, rank):
        return jnp.sum(jnp.where(expert[:, None] == expert_ids[None, :], pstart[None, :], 0), axis=1) + rank

    dest1 = slot(route_i[0], route_i[2])
    dest2 = slot(route_i[1], route_i[3])

    spare = (n_used[0] + jnp.arange(N_EXPERTS, dtype=jnp.int32)) * DISPATCH_BLOCK
    tail_block_start = jnp.concatenate([
        jnp.where(counts > 0, pend - DISPATCH_BLOCK, -1),
        jnp.where(spare < n_blocks * DISPATCH_BLOCK, spare, -1)]).astype(jnp.int32)
    x_buf = _dispatch(h2.reshape(T, D), dest1, dest2, tail_block_start, n_blocks * DISPATCH_BLOCK)
    y_buf = _experts(x_buf, block_expert, n_used, w_gate[0], w_up[0], w_down[0])
    return _combine(x1, route_w, mod3, final_norm_g.reshape(1, D), y_buf, dest1, dest2)
```
